```python
import jax
import jax.numpy as jnp
from jax import lax
import numpy as np

D_MODEL = 1024
BATCH = 4
SEQ = 8192
DEPTH = 2

CHUNK = 64
POOL_WIDTH = D_MODEL // 4
POOL_WINDOWS = (2, 4, 8, 16)
POOL_GROUP_DIM = POOL_WIDTH // len(POOL_WINDOWS)
CONV_WIDTH = D_MODEL // 4
CONV_KERNEL = 31
RWKV_WIDTH = D_MODEL - POOL_WIDTH - CONV_WIDTH
RWKV_HEAD_DIM = 64
RWKV_HEADS = RWKV_WIDTH // RWKV_HEAD_DIM
DECAY_RANK = 64
ICLR_RANK = 64
GATE_RANK = 128
VRES_RANK = 32
N_RWKV_COLS = 3 * RWKV_WIDTH + DECAY_RANK + ICLR_RANK + GATE_RANK
N_IN = POOL_WIDTH + 2 * CONV_WIDTH + N_RWKV_COLS
D_FF = ((8 * D_MODEL // 3 + 255) // 256) * 256
N_EXPERTS = 8
TOP_K = 2
D_FF_EXPERT = 7 * D_MODEL // 2
MOE_BLOCK = 256
N_DENSE = (DEPTH + 1) // 2
N_MOE = DEPTH // 2
RMS_EPS = 1e-6
LN_EPS = 1e-5
GN_EPS = 64e-5

kernel_name = 'hybrid_pool_conv_rwkv7_moe_block'


def rmsnorm(x, g):
    xf = x.astype(jnp.float32)
    y = xf * lax.rsqrt(jnp.mean(xf * xf, axis=-1, keepdims=True) + RMS_EPS)
    return (y * g.astype(jnp.float32)).astype(x.dtype)


def pool_mixer(u, w, scale):
    bsz, t, _ = u.shape
    uf = u.astype(jnp.float32)
    csum = jnp.pad(jnp.cumsum(uf, axis=1), ((0, 0), (1, 0), (0, 0)))
    pos = jnp.arange(1, t + 1, dtype=jnp.float32)[None, :, None]
    groups = []
    for gi, win in enumerate(POOL_WINDOWS):
        sl = slice(gi * POOL_GROUP_DIM, (gi + 1) * POOL_GROUP_DIM)
        c = csum[..., sl]
        lower = jnp.pad(c[:, : t - win + 1], ((0, 0), (win - 1, 0), (0, 0)))
        trailing_mean = (c[:, 1:] - lower) / jnp.minimum(pos, float(win))
        groups.append(trailing_mean - uf[..., sl])
    d = jnp.stack(groups, axis=2)
    y = jnp.einsum('btgc,gcd->btgd', d, w.astype(jnp.float32)).reshape(bsz, t, POOL_WIDTH)
    return (y * scale.astype(jnp.float32)).astype(u.dtype)


def conv_module(p, w, b, ln_g, ln_b):
    u = p[..., :CONV_WIDTH] * jax.nn.sigmoid(p[..., CONV_WIDTH:])
    y = lax.conv_general_dilated(u, w[:, None, :].astype(u.dtype), window_strides=(1,), padding=[(CONV_KERNEL - 1, 0)], dimension_numbers=('NWC', 'WIO', 'NWC'), feature_group_count=CONV_WIDTH)
    yf = y.astype(jnp.float32) + b.astype(jnp.float32)
    mu = jnp.mean(yf, axis=-1, keepdims=True)
    var = jnp.mean(jnp.square(yf - mu), axis=-1, keepdims=True)
    yf = (yf - mu) * lax.rsqrt(var + LN_EPS) * ln_g.astype(jnp.float32) + ln_b.astype(jnp.float32)
    return jax.nn.silu(yf).astype(p.dtype)


def rwkv7_scan(r, decay, k, v, kk, a):
    bsz, t, h, n = r.shape

    def to_chunks(z):
        return jnp.moveaxis(z, 1, 0).reshape(t // CHUNK, CHUNK, bsz, h, n)

    def frame_step(state, inp):
        r_t, w_t, k_t, v_t, kk_t, a_t = inp
        sa = jnp.einsum('bhvk,bhk->bhv', state, -kk_t)
        state = state * w_t[:, :, None, :] + sa[..., None] * (kk_t * a_t)[:, :, None, :] + v_t[..., None] * k_t[:, :, None, :]
        return state, jnp.einsum('bhvk,bhk->bhv', state, r_t)

    def chunk_step(state, chunk_inp):
        return lax.scan(frame_step, state, chunk_inp)

    s0 = jnp.zeros((bsz, h, n, n), jnp.float32)
    _, y = lax.scan(chunk_step, s0, tuple(to_chunks(z) for z in (r, decay, k, v, kk, a)))
    return jnp.moveaxis(y.reshape(t, bsz, h, n), 0, 1)


def rwkv7_mixer(c, mu, w0, w2, a0, a2, g2, k_k, k_a, r_k, gn_g, gn_b, v_first, vres):
    f32 = jnp.float32
    bsz, t, _ = c.shape
    cf = c.astype(f32)
    prev = jnp.pad(cf, ((0, 0), (1, 0), (0, 0)))[:, :-1]
    cf = cf + mu.astype(f32) * (prev - cf)
    wd = RWKV_WIDTH
    r = cf[..., :wd]
    k = cf[..., wd:2 * wd]
    v = cf[..., 2 * wd:3 * wd]
    o = 3 * wd
    xw = cf[..., o:o + DECAY_RANK]
    o += DECAY_RANK
    xa = cf[..., o:o + ICLR_RANK]
    o += ICLR_RANK
    xg = cf[..., o:o + GATE_RANK]
    w_log = -jax.nn.softplus(-(w0.astype(f32) + jnp.tanh(xw) @ w2.astype(f32))) - 0.5
    decay = jnp.exp(-jnp.exp(w_log))
    a = jax.nn.sigmoid(a0.astype(f32) + xa @ a2.astype(f32))
    g = jax.nn.sigmoid(xg) @ g2.astype(f32)
    if vres is not None:
        v0, v1, v2 = vres
        v = v + (v_first - v) * jax.nn.sigmoid(v0.astype(f32) + (v @ v1.astype(f32)) @ v2.astype(f32))

    def heads(z):
        return z.reshape(bsz, t, RWKV_HEADS, RWKV_HEAD_DIM)

    kk = heads(k * k_k.astype(f32))
    kk = kk / jnp.maximum(jnp.sqrt(jnp.sum(kk * kk, axis=-1, keepdims=True)), 1e-12)
    k = k * (1.0 + (a - 1.0) * k_a.astype(f32))
    rh, kh, vh = heads(r), heads(k), heads(v)
    y = rwkv7_scan(rh, heads(decay), kh, vh, kk, heads(a))
    ym = jnp.mean(y, axis=-1, keepdims=True)
    yv = jnp.mean(jnp.square(y - ym), axis=-1, keepdims=True)
    y = ((y - ym) * lax.rsqrt(yv + GN_EPS)).reshape(bsz, t, wd) * gn_g.astype(f32) + gn_b.astype(f32)
    y = heads(y) + jnp.sum(rh * kh * r_k.astype(f32), axis=-1, keepdims=True) * vh
    y = y.reshape(bsz, t, wd) * g
    return y.astype(c.dtype), v


def swiglu(x, w_gate, w_up, w_down):
    return (jax.nn.silu(x @ w_gate) * (x @ w_up)) @ w_down


def moe_swiglu(h, router, w_gate, w_up, w_down):
    bsz, t, d = h.shape
    n = bsz * t
    xf = h.reshape(n, d)
    logits = (xf @ router).astype(jnp.float32)
    top_val, top_idx = lax.top_k(logits, TOP_K)
    gates = jax.nn.softmax(top_val, axis=-1)
    flat_e = top_idx.reshape(-1).astype(jnp.int32)
    flat_tok = jnp.arange(n * TOP_K, dtype=jnp.int32) // TOP_K
    flat_g = gates.reshape(-1)
    order = jnp.argsort(flat_e)
    e_sorted = flat_e[order]
    tok_sorted = flat_tok[order]
    g_sorted = flat_g[order]
    counts = jnp.zeros((N_EXPERTS,), jnp.int32).at[flat_e].add(1)
    starts = jnp.cumsum(counts) - counts
    padded = (counts + MOE_BLOCK - 1) // MOE_BLOCK * MOE_BLOCK
    pad_ends = jnp.cumsum(padded)
    pad_starts = pad_ends - padded
    rank = jnp.arange(n * TOP_K, dtype=jnp.int32) - starts[e_sorted]
    dest = pad_starts[e_sorted] + rank
    n_blocks = (n * TOP_K + MOE_BLOCK - 1) // MOE_BLOCK + N_EXPERTS
    n_rows = n_blocks * MOE_BLOCK
    x_disp = jnp.zeros((n_rows, d), h.dtype).at[dest].set(xf[tok_sorted])
    block_e = jnp.minimum(jnp.searchsorted(pad_ends, jnp.arange(n_blocks, dtype=jnp.int32) * MOE_BLOCK, side='right'), N_EXPERTS - 1)

    def block_fn(args):
        xb, e = args
        return swiglu(xb, w_gate[e], w_up[e], w_down[e])

    y_disp = lax.map(block_fn, (x_disp.reshape(n_blocks, MOE_BLOCK, d), block_e)).reshape(n_rows, d)
    y = jnp.zeros((n, d), h.dtype).at[tok_sorted].add(y_disp[dest] * g_sorted[:, None].astype(h.dtype))
    return y.reshape(bsz, t, d)


def setup_inputs(seed: int = 0) -> dict:
    key = jax.random.key(seed)
    ks = iter(jax.random.split(key, 40))

    def nrm(shape, s):
        return jax.random.normal(next(ks), shape, jnp.float32) * s

    def uni(shape, lo, hi):
        return jax.random.uniform(next(ks), shape, jnp.float32, lo, hi)

    L = DEPTH
    LV = DEPTH - 1
    return {
        'x': nrm((BATCH, SEQ, D_MODEL), 1.0),
        'norm1_g': 1.0 + nrm((L, D_MODEL), 0.02),
        'w_in': nrm((L, D_MODEL, N_IN), D_MODEL ** -0.5),
        'pool_w': nrm((L, len(POOL_WINDOWS), POOL_GROUP_DIM, POOL_GROUP_DIM), POOL_GROUP_DIM ** -0.5),
        'pool_scale': 1.0 + nrm((L, POOL_WIDTH), 0.1),
        'conv_w': nrm((L, CONV_KERNEL, CONV_WIDTH), CONV_KERNEL ** -0.5),
        'conv_b': nrm((L, CONV_WIDTH), 0.02),
        'conv_ln_g': 1.0 + nrm((L, CONV_WIDTH), 0.02),
        'conv_ln_b': nrm((L, CONV_WIDTH), 0.02),
        'shift_mu': uni((L, N_RWKV_COLS), 0.0, 1.0),
        'rwkv_w0': uni((L, RWKV_WIDTH), -4.0, 0.0),
        'rwkv_w2': nrm((L, DECAY_RANK, RWKV_WIDTH), 0.5 * DECAY_RANK ** -0.5),
        'rwkv_a0': nrm((L, RWKV_WIDTH), 0.1),
        'rwkv_a2': nrm((L, ICLR_RANK, RWKV_WIDTH), 0.5 * ICLR_RANK ** -0.5),
        'rwkv_g2': nrm((L, GATE_RANK, RWKV_WIDTH), GATE_RANK ** -0.5),
        'rwkv_k_k': 0.85 + nrm((L, RWKV_WIDTH), 0.05),
        'rwkv_k_a': 1.0 + nrm((L, RWKV_WIDTH), 0.05),
        'rwkv_r_k': nrm((L, RWKV_HEADS, RWKV_HEAD_DIM), 0.1),
        'rwkv_gn_g': 1.0 + nrm((L, RWKV_WIDTH), 0.02),
        'rwkv_gn_b': nrm((L, RWKV_WIDTH), 0.02),
        'rwkv_v0': 1.0 + nrm((LV, RWKV_WIDTH), 0.1),
        'rwkv_v1': nrm((LV, RWKV_WIDTH, VRES_RANK), RWKV_WIDTH ** -0.5),
        'rwkv_v2': nrm((LV, VRES_RANK, RWKV_WIDTH), 0.5 * VRES_RANK ** -0.5),
        'w_out': nrm((L, D_MODEL, D_MODEL), D_MODEL ** -0.5),
        'norm2_g': 1.0 + nrm((L, D_MODEL), 0.02),
        'ffn_w_gate': nrm((N_DENSE, D_MODEL, D_FF), D_MODEL ** -0.5),
        'ffn_w_up': nrm((N_DENSE, D_MODEL, D_FF), D_MODEL ** -0.5),
        'ffn_w_down': nrm((N_DENSE, D_FF, D_MODEL), D_FF ** -0.5),
        'moe_router': nrm((N_MOE, D_MODEL, N_EXPERTS), D_MODEL ** -0.5),
        'moe_w_gate': nrm((N_MOE, N_EXPERTS, D_MODEL, D_FF_EXPERT), D_MODEL ** -0.5),
        'moe_w_up': nrm((N_MOE, N_EXPERTS, D_MODEL, D_FF_EXPERT), D_MODEL ** -0.5),
        'moe_w_down': nrm((N_MOE, N_EXPERTS, D_FF_EXPERT, D_MODEL), D_FF_EXPERT ** -0.5),
        'final_g': 1.0 + nrm((D_MODEL,), 0.02),
    }


def reference(x, norm1_g, w_in, pool_w, pool_scale, conv_w, conv_b, conv_ln_g, conv_ln_b, shift_mu, rwkv_w0, rwkv_w2, rwkv_a0, rwkv_a2, rwkv_g2, rwkv_k_k, rwkv_k_a, rwkv_r_k, rwkv_gn_g, rwkv_gn_b, rwkv_v0, rwkv_v1, rwkv_v2, w_out, norm2_g, ffn_w_gate, ffn_w_up, ffn_w_down, moe_router, moe_w_gate, moe_w_up, moe_w_down, final_g):
    h = x
    v_first = None
    for l in range(DEPTH):
        hn = rmsnorm(h, norm1_g[l])
        proj = hn @ w_in[l]
        p_pool = proj[..., :POOL_WIDTH]
        p_conv = proj[..., POOL_WIDTH:POOL_WIDTH + 2 * CONV_WIDTH]
        p_rwkv = proj[..., POOL_WIDTH + 2 * CONV_WIDTH:]
        y_pool = pool_mixer(p_pool, pool_w[l], pool_scale[l])
        y_conv = conv_module(p_conv, conv_w[l], conv_b[l], conv_ln_g[l], conv_ln_b[l])
        vres = None if l == 0 else (rwkv_v0[l - 1], rwkv_v1[l - 1], rwkv_v2[l - 1])
        y_rwkv, v_l = rwkv7_mixer(p_rwkv, shift_mu[l], rwkv_w0[l], rwkv_w2[l], rwkv_a0[l], rwkv_a2[l], rwkv_g2[l], rwkv_k_k[l], rwkv_k_a[l], rwkv_r_k[l], rwkv_gn_g[l], rwkv_gn_b[l], v_first, vres)
        if l == 0:
            v_first = v_l
        h = h + jnp.concatenate([y_pool, y_conv, y_rwkv], axis=-1) @ w_out[l]
        hn = rmsnorm(h, norm2_g[l])
        i = l // 2
        if l % 2 == 0:
            h = h + swiglu(hn, ffn_w_gate[i], ffn_w_up[i], ffn_w_down[i])
        else:
            h = h + moe_swiglu(hn, moe_router[i], moe_w_gate[i], moe_w_up[i], moe_w_down[i])
    return rmsnorm(h, final_g)
```

```python
import functools

import jax
import jax.numpy as jnp
from jax import lax
from jax.experimental import pallas as pl
from jax.experimental.pallas import tpu as pltpu

F32 = jnp.float32
BF16 = jnp.bfloat16

HEAD = 64
CHUNK = 64
POOL_WINDOWS = (2, 4, 8, 16)
CONV_KERNEL = 31
HALO = 32
RMS_EPS = 1e-6
LN_EPS = 1e-5
GN_EPS = 64e-5
N_TOP = 2
LANES = 128
VMEM_LIMIT = 56 * 1024 * 1024

NN = ((1,), (0,))
NT = ((1,), (1,))
TN = ((0,), (0,))


def _split(x, n):
    if x.dtype == BF16:
        return [x]
    parts = []
    rem = x
    for i in range(n):
        p = rem.astype(BF16)
        parts.append(p)
        if i + 1 < n:
            rem = rem - p.astype(F32)
    return parts


def _mm(a, b, dims=NN, na=1, nb=1):
    pa = _split(a, na)
    pb = _split(b, nb)
    nmax = max(len(pa), len(pb))
    out = None
    for i, x in enumerate(pa):
        for j, y in enumerate(pb):
            if i + j < nmax:
                t = lax.dot_general(x, y, (dims, ((), ())), preferred_element_type=F32)
                out = t if out is None else out + t
    return out


def _sigmoid(x):
    return 1.0 / (1.0 + jnp.exp(-x))


def _silu(x):
    return x * _sigmoid(x)


def _cparams(sem):
    return pltpu.CompilerParams(dimension_semantics=sem, vmem_limit_bytes=VMEM_LIMIT)


def _inproj_kernel(x_ref, g_ref, w_ref, op_ref, oc_ref, or_ref, *, n_pool, n_conv):
    x = x_ref[...]
    ms = jnp.mean(x * x, axis=-1, keepdims=True)
    y = x * lax.rsqrt(ms + RMS_EPS) * g_ref[...]
    p = jnp.dot(y.astype(BF16), w_ref[...], preferred_element_type=F32)
    op_ref[...] = p[:, :n_pool]
    oc_ref[...] = p[:, n_pool:n_pool + n_conv]
    or_ref[...] = p[:, n_pool + n_conv:]


def _inproj(h, g, w_bf16, n_pool, n_conv, tm):
    n, d = h.shape
    n_in = w_bf16.shape[1]
    n_rw = n_in - n_pool - n_conv
    return pl.pallas_call(
        functools.partial(_inproj_kernel, n_pool=n_pool, n_conv=n_conv),
        grid=(n // tm,),
        in_specs=[
            pl.BlockSpec((tm, d), lambda i: (i, 0)),
            pl.BlockSpec((1, d), lambda i: (0, 0)),
            pl.BlockSpec((d, n_in), lambda i: (0, 0)),
        ],
        out_specs=[
            pl.BlockSpec((tm, n_pool), lambda i: (i, 0)),
            pl.BlockSpec((tm, n_conv), lambda i: (i, 0)),
            pl.BlockSpec((tm, n_rw), lambda i: (i, 0)),
        ],
        out_shape=[
            jax.ShapeDtypeStruct((n, n_pool), F32),
            jax.ShapeDtypeStruct((n, n_conv), F32),
            jax.ShapeDtypeStruct((n, n_rw), F32),
        ],
        compiler_params=_cparams(("arbitrary",)),
        name="inproj",
    )(h, g.reshape(1, d), w_bf16)


def _poolconv_kernel(pp_ref, pc_ref, pw_ref, ps_ref, wl_ref, cw_ref, cb_ref, lg_ref, lb_ref,
                     o_ref, extp, extc, *, tt, cp, cc):
    j = pl.program_id(1)

    @pl.when(j == 0)
    def _():
        extp[0:HALO, :] = jnp.zeros((HALO, cp), F32)
        extc[0:HALO, :] = jnp.zeros((HALO, cc), F32)

    u = pp_ref[...]
    extp[HALO:HALO + tt, :] = u
    wl = wl_ref[...]
    run = u
    sel = jnp.zeros_like(u)
    k = 1
    for win in POOL_WINDOWS:
        while k < win:
            run = run + extp[pl.ds(HALO - k, tt), :]
            k += 1
        sel = jnp.where(wl == float(win), run, sel)
    pos = (j * tt + lax.broadcasted_iota(jnp.int32, (tt, 1), 0) + 1).astype(F32)
    d = sel / jnp.minimum(pos, wl) - u
    yp = _mm(d, pw_ref[...], NN, 2, 2) * ps_ref[...]
    extp[0:HALO, :] = extp[tt:tt + HALO, :]

    pc = pc_ref[...]
    uc = pc[:, :cc] * _sigmoid(pc[:, cc:])
    extc[HALO:HALO + tt, :] = uc
    acc = jnp.zeros((tt, cc), F32)
    for t in range(CONV_KERNEL):
        acc = acc + cw_ref[t:t + 1, :] * extc[pl.ds(HALO - (CONV_KERNEL - 1) + t, tt), :]
    extc[0:HALO, :] = extc[tt:tt + HALO, :]
    yf = acc + cb_ref[...]
    mu = jnp.mean(yf, axis=-1, keepdims=True)
    yc = yf - mu
    var = jnp.mean(yc * yc, axis=-1, keepdims=True)
    yn = yc * lax.rsqrt(var + LN_EPS) * lg_ref[...] + lb_ref[...]
    o_ref[:, :cp] = yp
    o_ref[:, cp:] = _silu(yn)


def _poolconv(pp, pc, pool_w, pool_scale, conv_w, conv_b, ln_g, ln_b, bsz, t, tt):
    n, cp = pp.shape
    cc = pc.shape[1] // 2
    ng, cg, _ = pool_w.shape
    pw = jnp.zeros((cp, cp), F32)
    for gi in range(ng):
        pw = pw.at[gi * cg:(gi + 1) * cg, gi * cg:(gi + 1) * cg].set(pool_w[gi])
    wl = jnp.repeat(jnp.asarray(POOL_WINDOWS, F32), cg).reshape(1, cp)
    nt = t // tt
    row = lambda b, j: (b * nt + j, 0)
    const = lambda b, j: (0, 0)
    return pl.pallas_call(
        functools.partial(_poolconv_kernel, tt=tt, cp=cp, cc=cc),
        grid=(bsz, nt),
        in_specs=[
            pl.BlockSpec((tt, cp), row),
            pl.BlockSpec((tt, 2 * cc), row),
            pl.BlockSpec((cp, cp), const),
            pl.BlockSpec((1, cp), const),
            pl.BlockSpec((1, cp), const),
            pl.BlockSpec((CONV_KERNEL, cc), const),
            pl.BlockSpec((1, cc), const),
            pl.BlockSpec((1, cc), const),
            pl.BlockSpec((1, cc), const),
        ],
        out_specs=pl.BlockSpec((tt, cp + cc), row),
        out_shape=jax.ShapeDtypeStruct((n, cp + cc), F32),
        scratch_shapes=[pltpu.VMEM((tt + HALO, cp), F32), pltpu.VMEM((tt + HALO, cc), F32)],
        compiler_params=_cparams(("arbitrary", "arbitrary")),
        name="poolconv",
    )(pp, pc, pw, pool_scale.reshape(1, cp), wl, conv_w, conv_b.reshape(1, cc),
      ln_g.reshape(1, cc), ln_b.reshape(1, cc))


def _rwkv_prep_kernel(*refs, tt, wd, r_dec, r_icl, r_gate, has_vres):
    it = iter(refs)
    c_ref = next(it)
    vf_ref = next(it) if has_vres else None
    mu_ref, w0_ref, w2_ref, a0_ref, a2_ref, g2_ref, kk_ref, ka_ref, rk_ref = (next(it) for _ in range(9))
    if has_vres:
        v0_ref, v1_ref, v2_ref = (next(it) for _ in range(3))
    ones_ref, tril_ref, blk_ref = next(it), next(it), next(it)
    m_ref, n_ref, q_ref, y0_ref, bonus_ref, g_ref, v_ref = (next(it) for _ in range(7))
    ext = next(it)
    s_at, s_rt, s_bt, s_kt, s_btp, s_ktp, s_pc, s_v = (next(it) for _ in range(8))

    j = pl.program_id(1)
    ncols = c_ref.shape[1]

    @pl.when(j == 0)
    def _():
        ext[0:8, :] = jnp.zeros((8, ncols), F32)

    c = c_ref[...]
    ext[8:8 + tt, :] = c
    prev = ext[pl.ds(7, tt), :]
    ext[0:8, :] = ext[tt:tt + 8, :]
    cf = c + mu_ref[...] * (prev - c)

    r = cf[:, :wd]
    k = cf[:, wd:2 * wd]
    v = cf[:, 2 * wd:3 * wd]
    o = 3 * wd
    xw = cf[:, o:o + r_dec]
    o += r_dec
    xa = cf[:, o:o + r_icl]
    o += r_icl
    xg = cf[:, o:o + r_gate]

    z = w0_ref[...] + _mm(jnp.tanh(xw), w2_ref[...], NN, 2, 2)
    sp = jnp.maximum(-z, 0.0) + jnp.log1p(jnp.exp(-jnp.abs(z)))
    w_log = -sp - 0.5
    lw = -jnp.exp(w_log)
    a = _sigmoid(a0_ref[...] + _mm(xa, a2_ref[...], NN, 2, 2))
    g = _mm(_sigmoid(xg), g2_ref[...], NN, 2, 2)
    if has_vres:
        lo = _mm(_mm(v, v1_ref[...], NN, 2, 2), v2_ref[...], NN, 2, 2)
        v = v + (vf_ref[...] - v) * _sigmoid(v0_ref[...] + lo)

    ones = ones_ref[...]
    kkp = k * kk_ref[...]
    nrm = jnp.sqrt(_mm(kkp * kkp, ones, NN, 2, 1))
    kk = kkp / jnp.maximum(nrm, 1e-12)
    k2 = k * (1.0 + (a - 1.0) * ka_ref[...])
    bonus_ref[...] = _mm(r * k2 * rk_ref[...], ones, NN, 2, 1) * v
    g_ref[...] = g
    v_ref[...] = v

    cl = _mm(tril_ref[...], lw, NN, 1, 3)
    lc = _mm(blk_ref[...], lw, NN, 1, 3)
    e_neg = jnp.exp(-cl)
    e_rem = jnp.exp(lc - cl)
    s_at[...] = -kk * jnp.exp(cl - lw)
    s_rt[...] = r * jnp.exp(cl)
    s_bt[...] = kk * a * e_neg
    s_kt[...] = k2 * e_neg
    s_btp[...] = kk * a * e_rem
    s_ktp[...] = k2 * e_rem
    s_pc[...] = jnp.exp(lc)
    s_v[...] = v

    ri = lax.broadcasted_iota(jnp.int32, (CHUNK, CHUNK), 0)
    ci = lax.broadcasted_iota(jnp.int32, (CHUNK, CHUNK), 1)
    strict = ri > ci
    incl = ri >= ci
    eye = (ri == ci).astype(F32)

    def chunk_body(ch, carry):
        rs = pl.ds(pl.multiple_of(ch * CHUNK, CHUNK), CHUNK)
        for h in range(wd // HEAD):
            hs = slice(h * HEAD, (h + 1) * HEAD)
            at, rt, bt, kt = s_at[rs, hs], s_rt[rs, hs], s_bt[rs, hs], s_kt[rs, hs]
            btp, ktp, vv = s_btp[rs, hs], s_ktp[rs, hs], s_v[rs, hs]
            aa = _mm(jnp.concatenate([at, rt], axis=0), jnp.concatenate([bt, kt], axis=0), NT, 2, 2)
            a_ab = jnp.where(strict, aa[:CHUNK, :CHUNK], 0.0)
            a_ak = jnp.where(strict, aa[:CHUNK, CHUNK:], 0.0)
            a_rb = jnp.where(incl, aa[CHUNK:, :CHUNK], 0.0)
            a_rk = jnp.where(incl, aa[CHUNK:, CHUNK:], 0.0)
            x = jnp.concatenate([at, _mm(a_ak, vv, NN, 2, 2)], axis=1)
            p = a_ab
            nsteps = CHUNK.bit_length() - 1
            for s in range(nsteps):
                x = x + _mm(p, x, NN, 2, 2)
                if s + 1 < nsteps:
                    p = _mm(p, p, NN, 2, 2)
            gt = _mm(btp, x, TN, 2, 2)
            gb = _mm(a_rb, x, NN, 2, 2)
            kv = _mm(ktp, vv, TN, 2, 2)
            yv = _mm(a_rk, vv, NN, 2, 2)
            m_ref[rs, hs] = eye * s_pc[rs, hs] + gt[:, :HEAD]
            n_ref[rs, hs] = gt[:, HEAD:] + kv
            q_ref[rs, hs] = rt + gb[:, :HEAD]
            y0_ref[rs, hs] = gb[:, HEAD:] + yv
        return carry

    lax.fori_loop(0, tt // CHUNK, chunk_body, 0)


def _rwkv_prep(c, v_first, p, vres, bsz, t, tt):
    n, ncols = c.shape
    wd = p["w0"].shape[0]
    r_dec, r_icl, r_gate = p["w2"].shape[0], p["a2"].shape[0], p["g2"].shape[0]
    has_vres = vres is not None
    nt = t // tt
    row = lambda b, j: (b * nt + j, 0)
    const = lambda b, j: (0, 0)
    vec = lambda a: a.reshape(1, -1)
    hid = jnp.arange(wd) // HEAD
    ones = (hid[:, None] == hid[None, :]).astype(BF16)
    ti = jnp.arange(tt)
    same = ti[:, None] // CHUNK == ti[None, :] // CHUNK
    tril = (same & (ti[:, None] >= ti[None, :])).astype(BF16)
    blk = same.astype(BF16)

    args = [c]
    specs = [pl.BlockSpec((tt, ncols), row)]
    if has_vres:
        args.append(v_first)
        specs.append(pl.BlockSpec((tt, wd), row))
    small = [vec(p["mu"]), vec(p["w0"]), p["w2"], vec(p["a0"]), p["a2"], p["g2"],
             vec(p["k_k"]), vec(p["k_a"]), vec(p["r_k"])]
    if has_vres:
        small += [vec(vres[0]), vres[1], vres[2]]
    small += [ones, tril, blk]
    for s in small:
        args.append(s)
        specs.append(pl.BlockSpec(s.shape, const))
    out_spec = pl.BlockSpec((tt, wd), row)
    out_sds = jax.ShapeDtypeStruct((n, wd), F32)
    return pl.pallas_call(
        functools.partial(_rwkv_prep_kernel, tt=tt, wd=wd, r_dec=r_dec, r_icl=r_icl,
                          r_gate=r_gate, has_vres=has_vres),
        grid=(bsz, nt),
        in_specs=specs,
        out_specs=[out_spec] * 7,
        out_shape=[out_sds] * 7,
        scratch_shapes=[pltpu.VMEM((tt + 8, ncols), F32)] + [pltpu.VMEM((tt, wd), F32)] * 8,
        compiler_params=_cparams(("arbitrary", "arbitrary")),
        name="rwkv_prep",
    )(*args)


def _rwkv_scan_kernel(m_ref, n_ref, q_ref, y0_ref, bonus_ref, g_ref, gg_ref, gb_ref, ones_ref,
                      o_ref, z_ref, y_ref, *, bsz, tt, wd):
    j = pl.program_id(0)
    nh = wd // HEAD

    @pl.when(j == 0)
    def _():
        z_ref[...] = jnp.zeros(z_ref.shape, F32)

    for ch in range(tt // CHUNK):
        rs = slice(ch * CHUNK, (ch + 1) * CHUNK)
        for b in range(bsz):
            for h in range(nh):
                hs = slice(h * HEAD, (h + 1) * HEAD)
                z = z_ref[b * nh + h]
                qm = jnp.concatenate([q_ref[b, rs, hs], m_ref[b, rs, hs]], axis=0)
                res = _mm(qm, z, NN, 2, 2)
                y_ref[b, rs, hs] = res[:CHUNK] + y0_ref[b, rs, hs]
                z_ref[b * nh + h] = res[CHUNK:] + n_ref[b, rs, hs]

    ones = ones_ref[...]
    for b in range(bsz):
        y = y_ref[b]
        ym = _mm(y, ones, NN, 2, 1) * (1.0 / HEAD)
        yc = y - ym
        yv = _mm(yc * yc, ones, NN, 2, 1) * (1.0 / HEAD)
        yn = yc * lax.rsqrt(yv + GN_EPS) * gg_ref[...] + gb_ref[...]
        o_ref[b] = (yn + bonus_ref[b]) * g_ref[b]


def _rwkv_scan(m, nn, q, y0, bonus, g, gn_g, gn_b, bsz, t, tt):
    n, wd = m.shape
    nh = wd // HEAD
    hid = jnp.arange(wd) // HEAD
    ones = (hid[:, None] == hid[None, :]).astype(BF16)
    r3 = lambda a: a.reshape(bsz, t, wd)
    blk = pl.BlockSpec((bsz, tt, wd), lambda j: (0, j, 0))
    const = lambda j: (0, 0)
    out = pl.pallas_call(
        functools.partial(_rwkv_scan_kernel, bsz=bsz, tt=tt, wd=wd),
        grid=(t // tt,),
        in_specs=[blk] * 6 + [pl.BlockSpec((1, wd), const), pl.BlockSpec((1, wd), const),
                              pl.BlockSpec((wd, wd), const)],
        out_specs=blk,
        out_shape=jax.ShapeDtypeStruct((bsz, t, wd), F32),
        scratch_shapes=[pltpu.VMEM((bsz * nh, HEAD, HEAD), F32), pltpu.VMEM((bsz, tt, wd), F32)],
        compiler_params=_cparams(("arbitrary",)),
        name="rwkv_scan",
    )(r3(m), r3(nn), r3(q), r3(y0), r3(bonus), r3(g), gn_g.reshape(1, wd), gn_b.reshape(1, wd), ones)
    return out.reshape(n, wd)


def _outproj_kernel(*refs, n_pc, route, n_exp):
    if route:
        (ypc_ref, yr_ref, h_ref, w_ref, g_ref, rt_ref, tri_ref,
         ho_ref, hn_ref, route_ref, cnt_ref, carry) = refs
    else:
        ypc_ref, yr_ref, h_ref, w_ref, g_ref, ho_ref, hn_ref = refs
    mix = (jnp.dot(ypc_ref[...].astype(BF16), w_ref[:n_pc, :], preferred_element_type=F32)
           + jnp.dot(yr_ref[...].astype(BF16), w_ref[n_pc:, :], preferred_element_type=F32))
    h = h_ref[...] + mix
    ho_ref[...] = h
    ms = jnp.mean(h * h, axis=-1, keepdims=True)
    hn = h * lax.rsqrt(ms + RMS_EPS) * g_ref[...]
    hn_ref[...] = hn.astype(hn_ref.dtype)
    if not route:
        return

    i = pl.program_id(0)

    @pl.when(i == 0)
    def _():
        carry[...] = jnp.zeros(carry.shape, F32)

    tm = hn.shape[0]
    logits = _mm(hn, rt_ref[...], NN, 3, 3)
    lane = lax.broadcasted_iota(jnp.int32, (tm, LANES), 1).astype(F32)
    neg = jnp.float32(-jnp.inf)
    logits = jnp.where(lane < n_exp, logits, neg)
    m1 = jnp.max(logits, axis=-1, keepdims=True)
    i1 = jnp.min(jnp.where(logits == m1, lane, float(LANES)), axis=-1, keepdims=True)
    rest = jnp.where(lane == i1, neg, logits)
    m2 = jnp.max(rest, axis=-1, keepdims=True)
    i2 = jnp.min(jnp.where(rest == m2, lane, float(LANES)), axis=-1, keepdims=True)
    e = jnp.exp(m2 - m1)
    g1 = 1.0 / (1.0 + e)
    g2 = e / (1.0 + e)
    oh1 = (lane == i1).astype(F32)
    oh2 = (lane == i2).astype(F32)
    both = oh1 + oh2
    before = carry[0:1, :] + _mm(tri_ref[...], both.astype(BF16), NN, 1, 1)
    rank1 = jnp.sum(before * oh1, axis=-1, keepdims=True)
    rank2 = jnp.sum(before * oh2, axis=-1, keepdims=True)
    tot = carry[0:1, :] + jnp.sum(both, axis=0, keepdims=True)
    carry[...] = jnp.broadcast_to(tot, carry.shape)
    cnt_ref[...] = jnp.broadcast_to(tot, cnt_ref.shape)
    out = jnp.where(lane == 0, g1, 0.0)
    out = jnp.where(lane == 1, g2, out)
    out = jnp.where(lane == 2, i1, out)
    out = jnp.where(lane == 3, i2, out)
    out = jnp.where(lane == 4, rank1, out)
    out = jnp.where(lane == 5, rank2, out)
    route_ref[...] = out


def _outproj(ypc, yr, h, w_bf16, g, router, tm, hn_dtype):
    n, d = h.shape
    n_pc, n_r = ypc.shape[1], yr.shape[1]
    route = router is not None
    rowb = lambda w: pl.BlockSpec((tm, w), lambda i: (i, 0))
    const = lambda i: (0, 0)
    args = [ypc, yr, h, w_bf16, g.reshape(1, d)]
    specs = [rowb(n_pc), rowb(n_r), rowb(d), pl.BlockSpec((d, d), const), pl.BlockSpec((1, d), const)]
    out_specs = [rowb(d), rowb(d)]
    out_shape = [jax.ShapeDtypeStruct((n, d), F32), jax.ShapeDtypeStruct((n, d), hn_dtype)]
    scratch = []
    n_exp = 0
    if route:
        n_exp = router.shape[1]
        rt = jnp.zeros((d, LANES), F32).at[:, :n_exp].set(router)
        ti = jnp.arange(tm)
        tri = (ti[:, None] > ti[None, :]).astype(BF16)
        args += [rt, tri]
        specs += [pl.BlockSpec((d, LANES), const), pl.BlockSpec((tm, tm), const)]
        out_specs += [rowb(LANES), pl.BlockSpec((8, LANES), const)]
        out_shape += [jax.ShapeDtypeStruct((n, LANES), F32), jax.ShapeDtypeStruct((8, LANES), F32)]
        scratch = [pltpu.VMEM((8, LANES), F32)]
    return pl.pallas_call(
        functools.partial(_outproj_kernel, n_pc=n_pc, route=route, n_exp=n_exp),
        grid=(n // tm,),
        in_specs=specs,
        out_specs=out_specs,
        out_shape=out_shape,
        scratch_shapes=scratch,
        compiler_params=_cparams(("arbitrary",)),
        name="outproj_route" if route else "outproj",
    )(*args)


def _ffn_kernel(be_ref, nu_ref, *refs, has_res):
    if has_res:
        x_ref, wg_ref, wu_ref, wd_ref, res_ref, o_ref, acc = refs
    else:
        x_ref, wg_ref, wu_ref, wd_ref, o_ref, acc = refs
    i = pl.program_id(0)
    f = pl.program_id(1)
    nf = pl.num_programs(1)
    used = i < nu_ref[0]

    @pl.when(used)
    def _():
        x = x_ref[...].astype(BF16)
        gate = jnp.dot(x, wg_ref[0], preferred_element_type=F32)
        up = jnp.dot(x, wu_ref[0], preferred_element_type=F32)
        mid = (_silu(gate) * up).astype(BF16)
        part = jnp.dot(mid, wd_ref[0], preferred_element_type=F32)

        @pl.when(f == 0)
        def _():
            acc[...] = part

        @pl.when(f > 0)
        def _():
            acc[...] += part

    @pl.when((f == nf - 1) & used)
    def _():
        o_ref[...] = acc[...] + res_ref[...] if has_res else acc[...]

    @pl.when((f == nf - 1) & jnp.logical_not(used))
    def _():
        o_ref[...] = res_ref[...] if has_res else jnp.zeros(o_ref.shape, F32)


def _ffn(x, wg, wu, wd, block_e, n_used, res, tr, tf):
    rows, d = x.shape
    _, _, fdim = wg.shape
    has_res = res is not None
    args = [x, wg, wu, wd]
    specs = [
        pl.BlockSpec((tr, d), lambda i, f, be, nu: (i, 0)),
        pl.BlockSpec((1, d, tf), lambda i, f, be, nu: (be[i], 0, f)),
        pl.BlockSpec((1, d, tf), lambda i, f, be, nu: (be[i], 0, f)),
        pl.BlockSpec((1, tf, d), lambda i, f, be, nu: (be[i], f, 0)),
    ]
    if has_res:
        args.append(res)
        specs.append(pl.BlockSpec((tr, d), lambda i, f, be, nu: (i, 0)))
    return pl.pallas_call(
        functools.partial(_ffn_kernel, has_res=has_res),
        grid_spec=pltpu.PrefetchScalarGridSpec(
            num_scalar_prefetch=2,
            grid=(rows // tr, fdim // tf),
            in_specs=specs,
            out_specs=pl.BlockSpec((tr, d), lambda i, f, be, nu: (i, 0)),
            scratch_shapes=[pltpu.VMEM((tr, d), F32)],
        ),
        out_shape=jax.ShapeDtypeStruct((rows, d), F32),
        compiler_params=_cparams(("arbitrary", "arbitrary")),
        name="ffn_res" if has_res else "ffn_grouped",
    )(block_e, n_used, *args)


def _row_copy(src_ref, s, dst_ref, d, sem):
    return pltpu.make_async_copy(src_ref.at[pl.ds(s, 1), :], dst_ref.at[pl.ds(d, 1), :], sem)


def _dispatch_kernel(dest_ref, x_ref, xd_in_ref, xd_ref, sem, *, tm):
    del xd_in_ref

    def issue(r, carry):
        for kk in range(N_TOP):
            _row_copy(x_ref, r, xd_ref, dest_ref[0, 0, N_TOP * r + kk], sem).start()
        return carry

    lax.fori_loop(0, tm, issue, 0)

    def drain(r, carry):
        for kk in range(N_TOP):
            _row_copy(x_ref, 0, xd_ref, 0, sem).wait()
        return carry

    lax.fori_loop(0, tm, drain, 0)


def _dispatch(x, dest, n_rows, tm):
    n, d = x.shape
    nblk = n // tm
    dest3 = dest.reshape(nblk, 1, N_TOP * tm)
    xd0 = jnp.zeros((n_rows, d), x.dtype)
    return pl.pallas_call(
        functools.partial(_dispatch_kernel, tm=tm),
        grid=(nblk,),
        in_specs=[
            pl.BlockSpec((1, 1, N_TOP * tm), lambda i: (i, 0, 0), memory_space=pltpu.SMEM),
            pl.BlockSpec((tm, d), lambda i: (i, 0)),
            pl.BlockSpec(memory_space=pl.ANY),
        ],
        out_specs=pl.BlockSpec(memory_space=pl.ANY),
        out_shape=jax.ShapeDtypeStruct((n_rows, d), x.dtype),
        scratch_shapes=[pltpu.SemaphoreType.DMA(())],
        input_output_aliases={2: 0},
        compiler_params=_cparams(("arbitrary",)),
        name="moe_dispatch",
    )(dest3, x, xd0)


def _combine_kernel(dest_ref, yd_ref, gate_ref, h_ref, fg_ref, o_ref, buf, sem, *, tm, final_norm):
    def issue(r, carry):
        for kk in range(N_TOP):
            _row_copy(yd_ref, dest_ref[0, 0, N_TOP * r + kk], buf.at[kk], r, sem).start()
        return carry

    lax.fori_loop(0, tm, issue, 0)

    def drain(r, carry):
        for kk in range(N_TOP):
            _row_copy(yd_ref, 0, buf.at[kk], 0, sem).wait()
        return carry

    lax.fori_loop(0, tm, drain, 0)

    gate = gate_ref[...]
    h = h_ref[...]
    for kk in range(N_TOP):
        h = h + buf[kk] * gate[:, kk:kk + 1]
    if final_norm:
        ms = jnp.mean(h * h, axis=-1, keepdims=True)
        h = h * lax.rsqrt(ms + RMS_EPS) * fg_ref[...]
    o_ref[...] = h


def _combine(yd, dest, gates, h, final_g, tm, final_norm):
    n, d = h.shape
    nblk = n // tm
    dest3 = dest.reshape(nblk, 1, N_TOP * tm)
    return pl.pallas_call(
        functools.partial(_combine_kernel, tm=tm, final_norm=final_norm),
        grid=(nblk,),
        in_specs=[
            pl.BlockSpec((1, 1, N_TOP * tm), lambda i: (i, 0, 0), memory_space=pltpu.SMEM),
            pl.BlockSpec(memory_space=pl.ANY),
            pl.BlockSpec((tm, LANES), lambda i: (i, 0)),
            pl.BlockSpec((tm, d), lambda i: (i, 0)),
            pl.BlockSpec((1, d), lambda i: (0, 0)),
        ],
        out_specs=pl.BlockSpec((tm, d), lambda i: (i, 0)),
        out_shape=jax.ShapeDtypeStruct((n, d), F32),
        scratch_shapes=[pltpu.VMEM((N_TOP, tm, d), F32), pltpu.SemaphoreType.DMA(())],
        compiler_params=_cparams(("arbitrary",)),
        name="moe_combine",
    )(dest3, yd, gates, h, final_g.reshape(1, d))


def _rmsnorm_kernel(x_ref, g_ref, o_ref):
    x = x_ref[...]
    ms = jnp.mean(x * x, axis=-1, keepdims=True)
    o_ref[...] = x * lax.rsqrt(ms + RMS_EPS) * g_ref[...]


def _rmsnorm(x, g, tm):
    n, d = x.shape
    return pl.pallas_call(
        _rmsnorm_kernel,
        grid=(n // tm,),
        in_specs=[pl.BlockSpec((tm, d), lambda i: (i, 0)), pl.BlockSpec((1, d), lambda i: (0, 0))],
        out_specs=pl.BlockSpec((tm, d), lambda i: (i, 0)),
        out_shape=jax.ShapeDtypeStruct((n, d), F32),
        compiler_params=_cparams(("arbitrary",)),
        name="final_rmsnorm",
    )(x, g.reshape(1, d))


def _pick(total, target, mult):
    best = None
    for cand in range(mult, min(total, target) + 1, mult):
        if total % cand == 0:
            best = cand
    assert best is not None, (total, target, mult)
    return best


def kernel(x, norm1_g, w_in, pool_w, pool_scale, conv_w, conv_b, conv_ln_g, conv_ln_b, shift_mu, rwkv_w0, rwkv_w2, rwkv_a0, rwkv_a2, rwkv_g2, rwkv_k_k, rwkv_k_a, rwkv_r_k, rwkv_gn_g, rwkv_gn_b, rwkv_v0, rwkv_v1, rwkv_v2, w_out, norm2_g, ffn_w_gate, ffn_w_up, ffn_w_down, moe_router, moe_w_gate, moe_w_up, moe_w_down, final_g):
    bsz, t, d = x.shape
    depth = w_in.shape[0]
    n = bsz * t
    n_pool = pool_scale.shape[1]
    n_conv = 2 * conv_b.shape[1]
    assert t % CHUNK == 0

    tm = _pick(n, 512, 8)
    tt_pc = _pick(t, 512, HALO)
    tt_prep = _pick(t, 256, CHUNK)
    tt_scan = _pick(t, 128, CHUNK)
    tm_moe = _pick(n, 256, 8)

    h = x.reshape(n, d)
    v_first = None
    for l in range(depth):
        pp, pc, prw = _inproj(h, norm1_g[l], w_in[l].astype(BF16), n_pool, n_conv, tm)
        ypc = _poolconv(pp, pc, pool_w[l], pool_scale[l], conv_w[l], conv_b[l],
                        conv_ln_g[l], conv_ln_b[l], bsz, t, tt_pc)
        rp = dict(mu=shift_mu[l], w0=rwkv_w0[l], w2=rwkv_w2[l], a0=rwkv_a0[l], a2=rwkv_a2[l],
                  g2=rwkv_g2[l], k_k=rwkv_k_k[l], k_a=rwkv_k_a[l], r_k=rwkv_r_k[l])
        vres = None if l == 0 else (rwkv_v0[l - 1], rwkv_v1[l - 1], rwkv_v2[l - 1])
        m, nn, q, y0, bonus, g, v_l = _rwkv_prep(prw, v_first, rp, vres, bsz, t, tt_prep)
        if l == 0:
            v_first = v_l
        yr = _rwkv_scan(m, nn, q, y0, bonus, g, rwkv_gn_g[l], rwkv_gn_b[l], bsz, t, tt_scan)

        i = l // 2
        last = l == depth - 1
        if l % 2 == 0:
            h_mid, hn = _outproj(ypc, yr, h, w_out[l].astype(BF16), norm2_g[l], None, tm, BF16)
            fdim = ffn_w_gate.shape[2]
            tr = _pick(n, 512, 8)
            tf = _pick(fdim, 1408, LANES)
            nblk = n // tr
            h = _ffn(hn, ffn_w_gate[i][None].astype(BF16), ffn_w_up[i][None].astype(BF16),
                     ffn_w_down[i][None].astype(BF16), jnp.zeros((nblk,), jnp.int32),
                     jnp.full((1,), nblk, jnp.int32), h_mid, tr, tf)
            if last:
                h = _rmsnorm(h, final_g, tm)
        else:
            n_exp = moe_router.shape[2]
            h_mid, hn, route, cnt = _outproj(ypc, yr, h, w_out[l].astype(BF16), norm2_g[l],
                                             moe_router[i], tm, F32)
            gates = route
            top_e = route[:, 2:2 + N_TOP].astype(jnp.int32)
            rank = route[:, 4:4 + N_TOP].astype(jnp.int32)
            counts = cnt[0, :n_exp].astype(jnp.int32)
            tr = _pick(n, 512, 8)
            padded = (counts + tr - 1) // tr * tr
            pad_ends = jnp.cumsum(padded)
            pad_starts = pad_ends - padded
            dest = pad_starts[top_e] + rank
            nblk = (n * N_TOP) // tr + n_exp
            block_e = jnp.minimum(
                jnp.searchsorted(pad_ends, jnp.arange(nblk, dtype=jnp.int32) * tr, side="right"),
                n_exp - 1).astype(jnp.int32)
            n_used = (pad_ends[-1] // tr).astype(jnp.int32).reshape(1)
            xd = _dispatch(hn, dest, nblk * tr, tm_moe)
            fdim = moe_w_gate.shape[3]
            tf = _pick(fdim, 896, LANES)
            yd = _ffn(xd, moe_w_gate[i].astype(BF16), moe_w_up[i].astype(BF16),
                      moe_w_down[i].astype(BF16), block_e, n_used, None, tr, tf)
            h = _combine(yd, dest, gates, h_mid, final_g, tm_moe, last)
    return h.reshape(bsz, t, d)
```

```python
import functools

import jax
import jax.numpy as jnp
from jax import lax
from jax.experimental import pallas as pl
from jax.experimental.pallas import tpu as pltpu

F32 = jnp.float32
BF16 = jnp.bfloat16

HEAD = 64
CHUNK = 64
POOL_WINDOWS = (2, 4, 8, 16)
CONV_KERNEL = 31
HALO = 32
RMS_EPS = 1e-6
LN_EPS = 1e-5
GN_EPS = 64e-5
N_TOP = 2
LANES = 128
VMEM_LIMIT = 56 * 1024 * 1024
SCAN_TERMS = 1
PREP_CHUNKS_PER_STEP = 2
DMA_UNROLL = 8

NN = ((1,), (0,))
NT = ((1,), (1,))
TN = ((0,), (0,))


def _split(x, n):
    if x.dtype == BF16:
        return [x]
    parts = []
    rem = x
    for i in range(n):
        p = rem.astype(BF16)
        parts.append(p)
        if i + 1 < n:
            rem = rem - p.astype(F32)
    return parts


def _mm(a, b, dims=NN, na=1, nb=1):
    pa = _split(a, na)
    pb = _split(b, nb)
    nmax = max(len(pa), len(pb))
    out = None
    for i, x in enumerate(pa):
        for j, y in enumerate(pb):
            if i + j < nmax:
                t = lax.dot_general(x, y, (dims, ((), ())), preferred_element_type=F32)
                out = t if out is None else out + t
    return out


def _sigmoid(x):
    return 1.0 / (1.0 + jnp.exp(-x))


def _silu(x):
    return x * _sigmoid(x)


def _cparams(sem):
    return pltpu.CompilerParams(dimension_semantics=sem, vmem_limit_bytes=VMEM_LIMIT)


def _inproj_kernel(x_ref, g_ref, w_ref, op_ref, oc_ref, or_ref, *, n_pool, n_conv):
    x = x_ref[...]
    ms = jnp.mean(x * x, axis=-1, keepdims=True)
    y = x * lax.rsqrt(ms + RMS_EPS) * g_ref[...]
    p = jnp.dot(y.astype(BF16), w_ref[...], preferred_element_type=F32)
    op_ref[...] = p[:, :n_pool]
    oc_ref[...] = p[:, n_pool:n_pool + n_conv]
    or_ref[...] = p[:, n_pool + n_conv:]


def _inproj(h, g, w_bf16, n_pool, n_conv, tm):
    n, d = h.shape
    n_in = w_bf16.shape[1]
    n_rw = n_in - n_pool - n_conv
    return pl.pallas_call(
        functools.partial(_inproj_kernel, n_pool=n_pool, n_conv=n_conv),
        grid=(n // tm,),
        in_specs=[
            pl.BlockSpec((tm, d), lambda i: (i, 0)),
            pl.BlockSpec((1, d), lambda i: (0, 0)),
            pl.BlockSpec((d, n_in), lambda i: (0, 0)),
        ],
        out_specs=[
            pl.BlockSpec((tm, n_pool), lambda i: (i, 0)),
            pl.BlockSpec((tm, n_conv), lambda i: (i, 0)),
            pl.BlockSpec((tm, n_rw), lambda i: (i, 0)),
        ],
        out_shape=[
            jax.ShapeDtypeStruct((n, n_pool), F32),
            jax.ShapeDtypeStruct((n, n_conv), F32),
            jax.ShapeDtypeStruct((n, n_rw), F32),
        ],
        compiler_params=_cparams(("arbitrary",)),
        name="inproj",
    )(h, g.reshape(1, d), w_bf16)


def _poolconv_kernel(pp_ref, pc_ref, pw_ref, ps_ref, wl_ref, cw_ref, cb_ref, lg_ref, lb_ref,
                     o_ref, extp, extc, *, tt, cp, cc):
    j = pl.program_id(1)

    @pl.when(j == 0)
    def _():
        extp[0:HALO, :] = jnp.zeros((HALO, cp), F32)
        extc[0:HALO, :] = jnp.zeros((HALO, cc), F32)

    u = pp_ref[...]
    extp[HALO:HALO + tt, :] = u
    wl = wl_ref[...]
    run = u
    sel = jnp.zeros_like(u)
    k = 1
    for win in POOL_WINDOWS:
        while k < win:
            run = run + extp[pl.ds(HALO - k, tt), :]
            k += 1
        sel = jnp.where(wl == float(win), run, sel)
    pos = (j * tt + lax.broadcasted_iota(jnp.int32, (tt, 1), 0) + 1).astype(F32)
    d = sel / jnp.minimum(pos, wl) - u
    yp = _mm(d, pw_ref[...], NN, 2, 2) * ps_ref[...]
    extp[0:HALO, :] = extp[tt:tt + HALO, :]

    pc = pc_ref[...]
    uc = pc[:, :cc] * _sigmoid(pc[:, cc:])
    extc[HALO:HALO + tt, :] = uc
    acc = jnp.zeros((tt, cc), F32)
    for t in range(CONV_KERNEL):
        acc = acc + cw_ref[t:t + 1, :] * extc[pl.ds(HALO - (CONV_KERNEL - 1) + t, tt), :]
    extc[0:HALO, :] = extc[tt:tt + HALO, :]
    yf = acc + cb_ref[...]
    mu = jnp.mean(yf, axis=-1, keepdims=True)
    yc = yf - mu
    var = jnp.mean(yc * yc, axis=-1, keepdims=True)
    yn = yc * lax.rsqrt(var + LN_EPS) * lg_ref[...] + lb_ref[...]
    o_ref[:, :cp] = yp
    o_ref[:, cp:] = _silu(yn)


def _poolconv(pp, pc, pool_w, pool_scale, conv_w, conv_b, ln_g, ln_b, bsz, t, tt):
    n, cp = pp.shape
    cc = pc.shape[1] // 2
    ng, cg, _ = pool_w.shape
    pw = jnp.zeros((cp, cp), F32)
    for gi in range(ng):
        pw = pw.at[gi * cg:(gi + 1) * cg, gi * cg:(gi + 1) * cg].set(pool_w[gi])
    wl = jnp.repeat(jnp.asarray(POOL_WINDOWS, F32), cg).reshape(1, cp)
    nt = t // tt
    row = lambda b, j: (b * nt + j, 0)
    const = lambda b, j: (0, 0)
    return pl.pallas_call(
        functools.partial(_poolconv_kernel, tt=tt, cp=cp, cc=cc),
        grid=(bsz, nt),
        in_specs=[
            pl.BlockSpec((tt, cp), row),
            pl.BlockSpec((tt, 2 * cc), row),
            pl.BlockSpec((cp, cp), const),
            pl.BlockSpec((1, cp), const),
            pl.BlockSpec((1, cp), const),
            pl.BlockSpec((CONV_KERNEL, cc), const),
            pl.BlockSpec((1, cc), const),
            pl.BlockSpec((1, cc), const),
            pl.BlockSpec((1, cc), const),
        ],
        out_specs=pl.BlockSpec((tt, cp + cc), row),
        out_shape=jax.ShapeDtypeStruct((n, cp + cc), F32),
        scratch_shapes=[pltpu.VMEM((tt + HALO, cp), F32), pltpu.VMEM((tt + HALO, cc), F32)],
        compiler_params=_cparams(("arbitrary", "arbitrary")),
        name="poolconv",
    )(pp, pc, pw, pool_scale.reshape(1, cp), wl, conv_w, conv_b.reshape(1, cc),
      ln_g.reshape(1, cc), ln_b.reshape(1, cc))


def _rwkv_prep_kernel(*refs, tt, wd, r_dec, r_icl, r_gate, has_vres):
    it = iter(refs)
    c_ref = next(it)
    vf_ref = next(it) if has_vres else None
    mu_ref, w0_ref, w2_ref, a0_ref, a2_ref, g2_ref, kk_ref, ka_ref, rk_ref = (next(it) for _ in range(9))
    if has_vres:
        v0_ref, v1_ref, v2_ref = (next(it) for _ in range(3))
    ones_ref, tril_ref, blk_ref = next(it), next(it), next(it)
    m_ref, n_ref, q_ref, y0_ref, bonus_ref, g_ref, v_ref = (next(it) for _ in range(7))
    ext = next(it)
    s_at, s_rt, s_bt, s_kt, s_btp, s_ktp, s_pc, s_v = (next(it) for _ in range(8))

    j = pl.program_id(1)
    ncols = c_ref.shape[1]

    @pl.when(j == 0)
    def _():
        ext[0:8, :] = jnp.zeros((8, ncols), F32)

    c = c_ref[...]
    ext[8:8 + tt, :] = c
    prev = ext[pl.ds(7, tt), :]
    ext[0:8, :] = ext[tt:tt + 8, :]
    cf = c + mu_ref[...] * (prev - c)

    r = cf[:, :wd]
    k = cf[:, wd:2 * wd]
    v = cf[:, 2 * wd:3 * wd]
    o = 3 * wd
    xw = cf[:, o:o + r_dec]
    o += r_dec
    xa = cf[:, o:o + r_icl]
    o += r_icl
    xg = cf[:, o:o + r_gate]

    z = w0_ref[...] + _mm(jnp.tanh(xw), w2_ref[...], NN, 2, 2)
    sp = jnp.maximum(-z, 0.0) + jnp.log1p(jnp.exp(-jnp.abs(z)))
    w_log = -sp - 0.5
    lw = -jnp.exp(w_log)
    a = _sigmoid(a0_ref[...] + _mm(xa, a2_ref[...], NN, 2, 2))
    g = _mm(_sigmoid(xg), g2_ref[...], NN, 2, 2)
    if has_vres:
        lo = _mm(_mm(v, v1_ref[...], NN, 2, 2), v2_ref[...], NN, 2, 2)
        v = v + (vf_ref[...] - v) * _sigmoid(v0_ref[...] + lo)

    ones = ones_ref[...]
    kkp = k * kk_ref[...]
    nrm = jnp.sqrt(_mm(kkp * kkp, ones, NN, 2, 1))
    kk = kkp / jnp.maximum(nrm, 1e-12)
    k2 = k * (1.0 + (a - 1.0) * ka_ref[...])
    bonus_ref[...] = _mm(r * k2 * rk_ref[...], ones, NN, 2, 1) * v
    g_ref[...] = g
    v_ref[...] = v

    cl = _mm(tril_ref[...], lw, NN, 1, 3)
    lc = _mm(blk_ref[...], lw, NN, 1, 3)
    e_neg = jnp.exp(-cl)
    e_rem = jnp.exp(lc - cl)
    s_at[...] = -kk * jnp.exp(cl - lw)
    s_rt[...] = r * jnp.exp(cl)
    s_bt[...] = kk * a * e_neg
    s_kt[...] = k2 * e_neg
    s_btp[...] = kk * a * e_rem
    s_ktp[...] = k2 * e_rem
    s_pc[...] = jnp.exp(lc)
    s_v[...] = v

    ri = lax.broadcasted_iota(jnp.int32, (CHUNK, CHUNK), 0)
    ci = lax.broadcasted_iota(jnp.int32, (CHUNK, CHUNK), 1)
    strict = ri > ci
    incl = ri >= ci
    eye = (ri == ci).astype(F32)

    def chunk_body(it, carry):
        units = []
        for cc in range(PREP_CHUNKS_PER_STEP):
            rs = pl.ds(pl.multiple_of((it * PREP_CHUNKS_PER_STEP + cc) * CHUNK, CHUNK), CHUNK)
            units += [(rs, slice(h * HEAD, (h + 1) * HEAD)) for h in range(wd // HEAD)]
        nu = len(units)
        sp = SCAN_TERMS
        at = [s_at[u] for u in units]
        rt = [s_rt[u] for u in units]
        vv = [s_v[u] for u in units]
        aa = [_mm(jnp.concatenate([at[i], rt[i]], axis=0),
                  jnp.concatenate([s_bt[u], s_kt[u]], axis=0), NT, sp, sp)
              for i, u in enumerate(units)]
        p = [jnp.where(strict, a[:CHUNK, :CHUNK], 0.0) for a in aa]
        a_ak = [jnp.where(strict, a[:CHUNK, CHUNK:], 0.0) for a in aa]
        a_rb = [jnp.where(incl, a[CHUNK:, :CHUNK], 0.0) for a in aa]
        a_rk = [jnp.where(incl, a[CHUNK:, CHUNK:], 0.0) for a in aa]
        x = [jnp.concatenate([at[i], _mm(a_ak[i], vv[i], NN, sp, sp)], axis=1) for i in range(nu)]
        nsteps = CHUNK.bit_length() - 1
        for s in range(nsteps):
            x = [xi + _mm(pi, xi, NN, sp, sp) for pi, xi in zip(p, x)]
            if s + 1 < nsteps:
                p = [_mm(pi, pi, NN, sp, sp) for pi in p]
        gt = [_mm(s_btp[u], x[i], TN, sp, sp) for i, u in enumerate(units)]
        gb = [_mm(a_rb[i], x[i], NN, sp, sp) for i in range(nu)]
        kv = [_mm(s_ktp[u], vv[i], TN, sp, sp) for i, u in enumerate(units)]
        yv = [_mm(a_rk[i], vv[i], NN, sp, sp) for i in range(nu)]
        for i, u in enumerate(units):
            m_ref[u] = eye * s_pc[u] + gt[i][:, :HEAD]
            n_ref[u] = gt[i][:, HEAD:] + kv[i]
            q_ref[u] = rt[i] + gb[i][:, :HEAD]
            y0_ref[u] = gb[i][:, HEAD:] + yv[i]
        return carry

    lax.fori_loop(0, tt // (CHUNK * PREP_CHUNKS_PER_STEP), chunk_body, 0)


def _rwkv_prep(c, v_first, p, vres, bsz, t, tt):
    n, ncols = c.shape
    wd = p["w0"].shape[0]
    r_dec, r_icl, r_gate = p["w2"].shape[0], p["a2"].shape[0], p["g2"].shape[0]
    has_vres = vres is not None
    nt = t // tt
    row = lambda b, j: (b * nt + j, 0)
    const = lambda b, j: (0, 0)
    vec = lambda a: a.reshape(1, -1)
    hid = jnp.arange(wd) // HEAD
    ones = (hid[:, None] == hid[None, :]).astype(BF16)
    ti = jnp.arange(tt)
    same = ti[:, None] // CHUNK == ti[None, :] // CHUNK
    tril = (same & (ti[:, None] >= ti[None, :])).astype(BF16)
    blk = same.astype(BF16)

    args = [c]
    specs = [pl.BlockSpec((tt, ncols), row)]
    if has_vres:
        args.append(v_first)
        specs.append(pl.BlockSpec((tt, wd), row))
    small = [vec(p["mu"]), vec(p["w0"]), p["w2"], vec(p["a0"]), p["a2"], p["g2"],
             vec(p["k_k"]), vec(p["k_a"]), vec(p["r_k"])]
    if has_vres:
        small += [vec(vres[0]), vres[1], vres[2]]
    small += [ones, tril, blk]
    for s in small:
        args.append(s)
        specs.append(pl.BlockSpec(s.shape, const))
    out_spec = pl.BlockSpec((tt, wd), row)
    out_sds = jax.ShapeDtypeStruct((n, wd), F32)
    return pl.pallas_call(
        functools.partial(_rwkv_prep_kernel, tt=tt, wd=wd, r_dec=r_dec, r_icl=r_icl,
                          r_gate=r_gate, has_vres=has_vres),
        grid=(bsz, nt),
        in_specs=specs,
        out_specs=[out_spec] * 7,
        out_shape=[out_sds] * 7,
        scratch_shapes=[pltpu.VMEM((tt + 8, ncols), F32)] + [pltpu.VMEM((tt, wd), F32)] * 8,
        compiler_params=_cparams(("arbitrary", "arbitrary")),
        name="rwkv_prep",
    )(*args)


def _rwkv_scan_kernel(m_ref, n_ref, q_ref, y0_ref, bonus_ref, g_ref, gg_ref, gb_ref, ones_ref,
                      o_ref, z_ref, y_ref, *, bsz, tt, wd):
    j = pl.program_id(0)
    nh = wd // HEAD

    @pl.when(j == 0)
    def _():
        z_ref[...] = jnp.zeros(z_ref.shape, F32)

    for ch in range(tt // CHUNK):
        rs = slice(ch * CHUNK, (ch + 1) * CHUNK)
        for b in range(bsz):
            for h in range(nh):
                hs = slice(h * HEAD, (h + 1) * HEAD)
                z = z_ref[b * nh + h]
                qm = jnp.concatenate([q_ref[b, rs, hs], m_ref[b, rs, hs]], axis=0)
                res = _mm(qm, z, NN, 2, 2)
                y_ref[b, rs, hs] = res[:CHUNK] + y0_ref[b, rs, hs]
                z_ref[b * nh + h] = res[CHUNK:] + n_ref[b, rs, hs]

    ones = ones_ref[...]
    for b in range(bsz):
        y = y_ref[b]
        ym = _mm(y, ones, NN, 2, 1) * (1.0 / HEAD)
        yc = y - ym
        yv = _mm(yc * yc, ones, NN, 2, 1) * (1.0 / HEAD)
        yn = yc * lax.rsqrt(yv + GN_EPS) * gg_ref[...] + gb_ref[...]
        o_ref[b] = (yn + bonus_ref[b]) * g_ref[b]


def _rwkv_scan(m, nn, q, y0, bonus, g, gn_g, gn_b, bsz, t, tt):
    n, wd = m.shape
    nh = wd // HEAD
    hid = jnp.arange(wd) // HEAD
    ones = (hid[:, None] == hid[None, :]).astype(BF16)
    r3 = lambda a: a.reshape(bsz, t, wd)
    blk = pl.BlockSpec((bsz, tt, wd), lambda j: (0, j, 0))
    const = lambda j: (0, 0)
    out = pl.pallas_call(
        functools.partial(_rwkv_scan_kernel, bsz=bsz, tt=tt, wd=wd),
        grid=(t // tt,),
        in_specs=[blk] * 6 + [pl.BlockSpec((1, wd), const), pl.BlockSpec((1, wd), const),
                              pl.BlockSpec((wd, wd), const)],
        out_specs=blk,
        out_shape=jax.ShapeDtypeStruct((bsz, t, wd), F32),
        scratch_shapes=[pltpu.VMEM((bsz * nh, HEAD, HEAD), F32), pltpu.VMEM((bsz, tt, wd), F32)],
        compiler_params=_cparams(("arbitrary",)),
        name="rwkv_scan",
    )(r3(m), r3(nn), r3(q), r3(y0), r3(bonus), r3(g), gn_g.reshape(1, wd), gn_b.reshape(1, wd), ones)
    return out.reshape(n, wd)


def _outproj_kernel(*refs, n_pc, route, n_exp):
    if route:
        (ypc_ref, yr_ref, h_ref, w_ref, g_ref, rt_ref, tri_ref,
         ho_ref, hn_ref, route_ref, cnt_ref, carry) = refs
    else:
        ypc_ref, yr_ref, h_ref, w_ref, g_ref, ho_ref, hn_ref = refs
    mix = (jnp.dot(ypc_ref[...].astype(BF16), w_ref[:n_pc, :], preferred_element_type=F32)
           + jnp.dot(yr_ref[...].astype(BF16), w_ref[n_pc:, :], preferred_element_type=F32))
    h = h_ref[...] + mix
    ho_ref[...] = h
    ms = jnp.mean(h * h, axis=-1, keepdims=True)
    hn = h * lax.rsqrt(ms + RMS_EPS) * g_ref[...]
    hn_ref[...] = hn.astype(hn_ref.dtype)
    if not route:
        return

    i = pl.program_id(0)

    @pl.when(i == 0)
    def _():
        carry[...] = jnp.zeros(carry.shape, F32)

    tm = hn.shape[0]
    logits = _mm(hn, rt_ref[...], NN, 3, 3)
    lane = lax.broadcasted_iota(jnp.int32, (tm, LANES), 1).astype(F32)
    neg = jnp.float32(-jnp.inf)
    logits = jnp.where(lane < n_exp, logits, neg)
    m1 = jnp.max(logits, axis=-1, keepdims=True)
    i1 = jnp.min(jnp.where(logits == m1, lane, float(LANES)), axis=-1, keepdims=True)
    rest = jnp.where(lane == i1, neg, logits)
    m2 = jnp.max(rest, axis=-1, keepdims=True)
    i2 = jnp.min(jnp.where(rest == m2, lane, float(LANES)), axis=-1, keepdims=True)
    e = jnp.exp(m2 - m1)
    g1 = 1.0 / (1.0 + e)
    g2 = e / (1.0 + e)
    oh1 = (lane == i1).astype(F32)
    oh2 = (lane == i2).astype(F32)
    both = oh1 + oh2
    before = carry[0:1, :] + _mm(tri_ref[...], both.astype(BF16), NN, 1, 1)
    rank1 = jnp.sum(before * oh1, axis=-1, keepdims=True)
    rank2 = jnp.sum(before * oh2, axis=-1, keepdims=True)
    tot = carry[0:1, :] + jnp.sum(both, axis=0, keepdims=True)
    carry[...] = jnp.broadcast_to(tot, carry.shape)
    cnt_ref[...] = jnp.broadcast_to(tot, cnt_ref.shape)
    out = jnp.where(lane == 0, g1, 0.0)
    out = jnp.where(lane == 1, g2, out)
    out = jnp.where(lane == 2, i1, out)
    out = jnp.where(lane == 3, i2, out)
    out = jnp.where(lane == 4, rank1, out)
    out = jnp.where(lane == 5, rank2, out)
    route_ref[...] = out


def _outproj(ypc, yr, h, w_bf16, g, router, tm, hn_dtype):
    n, d = h.shape
    n_pc, n_r = ypc.shape[1], yr.shape[1]
    route = router is not None
    rowb = lambda w: pl.BlockSpec((tm, w), lambda i: (i, 0))
    const = lambda i: (0, 0)
    args = [ypc, yr, h, w_bf16, g.reshape(1, d)]
    specs = [rowb(n_pc), rowb(n_r), rowb(d), pl.BlockSpec((d, d), const), pl.BlockSpec((1, d), const)]
    out_specs = [rowb(d), rowb(d)]
    out_shape = [jax.ShapeDtypeStruct((n, d), F32), jax.ShapeDtypeStruct((n, d), hn_dtype)]
    scratch = []
    n_exp = 0
    if route:
        n_exp = router.shape[1]
        rt = jnp.zeros((d, LANES), F32).at[:, :n_exp].set(router)
        ti = jnp.arange(tm)
        tri = (ti[:, None] > ti[None, :]).astype(BF16)
        args += [rt, tri]
        specs += [pl.BlockSpec((d, LANES), const), pl.BlockSpec((tm, tm), const)]
        out_specs += [rowb(LANES), pl.BlockSpec((8, LANES), const)]
        out_shape += [jax.ShapeDtypeStruct((n, LANES), F32), jax.ShapeDtypeStruct((8, LANES), F32)]
        scratch = [pltpu.VMEM((8, LANES), F32)]
    return pl.pallas_call(
        functools.partial(_outproj_kernel, n_pc=n_pc, route=route, n_exp=n_exp),
        grid=(n // tm,),
        in_specs=specs,
        out_specs=out_specs,
        out_shape=out_shape,
        scratch_shapes=scratch,
        compiler_params=_cparams(("arbitrary",)),
        name="outproj_route" if route else "outproj",
    )(*args)


def _ffn_kernel(be_ref, nu_ref, *refs, has_res):
    if has_res:
        x_ref, wg_ref, wu_ref, wd_ref, res_ref, o_ref, acc = refs
    else:
        x_ref, wg_ref, wu_ref, wd_ref, o_ref, acc = refs
    i = pl.program_id(0)
    f = pl.program_id(1)
    nf = pl.num_programs(1)
    used = i < nu_ref[0]

    @pl.when(used)
    def _():
        x = x_ref[...].astype(BF16)
        gate = jnp.dot(x, wg_ref[0], preferred_element_type=F32)
        up = jnp.dot(x, wu_ref[0], preferred_element_type=F32)
        mid = (_silu(gate) * up).astype(BF16)
        part = jnp.dot(mid, wd_ref[0], preferred_element_type=F32)

        @pl.when(f == 0)
        def _():
            acc[...] = part

        @pl.when(f > 0)
        def _():
            acc[...] += part

    @pl.when((f == nf - 1) & used)
    def _():
        o_ref[...] = acc[...] + res_ref[...] if has_res else acc[...]

    @pl.when((f == nf - 1) & jnp.logical_not(used))
    def _():
        o_ref[...] = res_ref[...] if has_res else jnp.zeros(o_ref.shape, F32)


def _ffn(x, wg, wu, wd, block_e, n_used, res, tr, tf):
    rows, d = x.shape
    _, _, fdim = wg.shape
    has_res = res is not None
    args = [x, wg, wu, wd]
    specs = [
        pl.BlockSpec((tr, d), lambda i, f, be, nu: (i, 0)),
        pl.BlockSpec((1, d, tf), lambda i, f, be, nu: (be[i], 0, f)),
        pl.BlockSpec((1, d, tf), lambda i, f, be, nu: (be[i], 0, f)),
        pl.BlockSpec((1, tf, d), lambda i, f, be, nu: (be[i], f, 0)),
    ]
    if has_res:
        args.append(res)
        specs.append(pl.BlockSpec((tr, d), lambda i, f, be, nu: (i, 0)))
    return pl.pallas_call(
        functools.partial(_ffn_kernel, has_res=has_res),
        grid_spec=pltpu.PrefetchScalarGridSpec(
            num_scalar_prefetch=2,
            grid=(rows // tr, fdim // tf),
            in_specs=specs,
            out_specs=pl.BlockSpec((tr, d), lambda i, f, be, nu: (i, 0)),
            scratch_shapes=[pltpu.VMEM((tr, d), F32)],
        ),
        out_shape=jax.ShapeDtypeStruct((rows, d), F32),
        compiler_params=_cparams(("arbitrary", "arbitrary")),
        name="ffn_res" if has_res else "ffn_grouped",
    )(block_e, n_used, *args)


def _row_copy(src_ref, s, dst_ref, d, sem):
    return pltpu.make_async_copy(src_ref.at[pl.ds(s, 1), :], dst_ref.at[pl.ds(d, 1), :], sem)


def _dispatch_kernel(dest_ref, x_ref, xd_in_ref, xd_ref, sem, *, tm):
    del xd_in_ref

    def issue(r, carry):
        for kk in range(N_TOP):
            _row_copy(x_ref, r, xd_ref, dest_ref[0, 0, N_TOP * r + kk], sem).start()
        return carry

    lax.fori_loop(0, tm, issue, 0, unroll=DMA_UNROLL)

    def drain(r, carry):
        for kk in range(N_TOP):
            _row_copy(x_ref, 0, xd_ref, 0, sem).wait()
        return carry

    lax.fori_loop(0, tm, drain, 0, unroll=DMA_UNROLL)


def _dispatch(x, dest, n_rows, tm):
    n, d = x.shape
    nblk = n // tm
    dest3 = dest.reshape(nblk, 1, N_TOP * tm)
    xd0 = jnp.zeros((n_rows, d), x.dtype)
    return pl.pallas_call(
        functools.partial(_dispatch_kernel, tm=tm),
        grid=(nblk,),
        in_specs=[
            pl.BlockSpec((1, 1, N_TOP * tm), lambda i: (i, 0, 0), memory_space=pltpu.SMEM),
            pl.BlockSpec((tm, d), lambda i: (i, 0)),
            pl.BlockSpec(memory_space=pl.ANY),
        ],
        out_specs=pl.BlockSpec(memory_space=pl.ANY),
        out_shape=jax.ShapeDtypeStruct((n_rows, d), x.dtype),
        scratch_shapes=[pltpu.SemaphoreType.DMA(())],
        input_output_aliases={2: 0},
        compiler_params=_cparams(("arbitrary",)),
        name="moe_dispatch",
    )(dest3, x, xd0)


def _combine_kernel(dest_ref, yd_ref, gate_ref, h_ref, fg_ref, o_ref, buf, sem, *, tm, final_norm):
    def issue(r, carry):
        for kk in range(N_TOP):
            _row_copy(yd_ref, dest_ref[0, 0, N_TOP * r + kk], buf.at[kk], r, sem).start()
        return carry

    lax.fori_loop(0, tm, issue, 0, unroll=DMA_UNROLL)

    def drain(r, carry):
        for kk in range(N_TOP):
            _row_copy(yd_ref, 0, buf.at[kk], 0, sem).wait()
        return carry

    lax.fori_loop(0, tm, drain, 0, unroll=DMA_UNROLL)

    gate = gate_ref[...]
    h = h_ref[...]
    for kk in range(N_TOP):
        h = h + buf[kk] * gate[:, kk:kk + 1]
    if final_norm:
        ms = jnp.mean(h * h, axis=-1, keepdims=True)
        h = h * lax.rsqrt(ms + RMS_EPS) * fg_ref[...]
    o_ref[...] = h


def _combine(yd, dest, gates, h, final_g, tm, final_norm):
    n, d = h.shape
    nblk = n // tm
    dest3 = dest.reshape(nblk, 1, N_TOP * tm)
    return pl.pallas_call(
        functools.partial(_combine_kernel, tm=tm, final_norm=final_norm),
        grid=(nblk,),
        in_specs=[
            pl.BlockSpec((1, 1, N_TOP * tm), lambda i: (i, 0, 0), memory_space=pltpu.SMEM),
            pl.BlockSpec(memory_space=pl.ANY),
            pl.BlockSpec((tm, LANES), lambda i: (i, 0)),
            pl.BlockSpec((tm, d), lambda i: (i, 0)),
            pl.BlockSpec((1, d), lambda i: (0, 0)),
        ],
        out_specs=pl.BlockSpec((tm, d), lambda i: (i, 0)),
        out_shape=jax.ShapeDtypeStruct((n, d), F32),
        scratch_shapes=[pltpu.VMEM((N_TOP, tm, d), F32), pltpu.SemaphoreType.DMA(())],
        compiler_params=_cparams(("arbitrary",)),
        name="moe_combine",
    )(dest3, yd, gates, h, final_g.reshape(1, d))


def _rmsnorm_kernel(x_ref, g_ref, o_ref):
    x = x_ref[...]
    ms = jnp.mean(x * x, axis=-1, keepdims=True)
    o_ref[...] = x * lax.rsqrt(ms + RMS_EPS) * g_ref[...]


def _rmsnorm(x, g, tm):
    n, d = x.shape
    return pl.pallas_call(
        _rmsnorm_kernel,
        grid=(n // tm,),
        in_specs=[pl.BlockSpec((tm, d), lambda i: (i, 0)), pl.BlockSpec((1, d), lambda i: (0, 0))],
        out_specs=pl.BlockSpec((tm, d), lambda i: (i, 0)),
        out_shape=jax.ShapeDtypeStruct((n, d), F32),
        compiler_params=_cparams(("arbitrary",)),
        name="final_rmsnorm",
    )(x, g.reshape(1, d))


def _pick(total, target, mult):
    best = None
    for cand in range(mult, min(total, target) + 1, mult):
        if total % cand == 0:
            best = cand
    assert best is not None, (total, target, mult)
    return best


def kernel(x, norm1_g, w_in, pool_w, pool_scale, conv_w, conv_b, conv_ln_g, conv_ln_b, shift_mu, rwkv_w0, rwkv_w2, rwkv_a0, rwkv_a2, rwkv_g2, rwkv_k_k, rwkv_k_a, rwkv_r_k, rwkv_gn_g, rwkv_gn_b, rwkv_v0, rwkv_v1, rwkv_v2, w_out, norm2_g, ffn_w_gate, ffn_w_up, ffn_w_down, moe_router, moe_w_gate, moe_w_up, moe_w_down, final_g):
    bsz, t, d = x.shape
    depth = w_in.shape[0]
    n = bsz * t
    n_pool = pool_scale.shape[1]
    n_conv = 2 * conv_b.shape[1]
    assert t % CHUNK == 0

    tm = _pick(n, 512, 8)
    tt_pc = _pick(t, 512, HALO)
    tt_prep = _pick(t, 256, CHUNK)
    tt_scan = _pick(t, 128, CHUNK)
    tm_moe = _pick(n, 256, 8)

    h = x.reshape(n, d)
    v_first = None
    for l in range(depth):
        pp, pc, prw = _inproj(h, norm1_g[l], w_in[l].astype(BF16), n_pool, n_conv, tm)
        ypc = _poolconv(pp, pc, pool_w[l], pool_scale[l], conv_w[l], conv_b[l],
                        conv_ln_g[l], conv_ln_b[l], bsz, t, tt_pc)
        rp = dict(mu=shift_mu[l], w0=rwkv_w0[l], w2=rwkv_w2[l], a0=rwkv_a0[l], a2=rwkv_a2[l],
                  g2=rwkv_g2[l], k_k=rwkv_k_k[l], k_a=rwkv_k_a[l], r_k=rwkv_r_k[l])
        vres = None if l == 0 else (rwkv_v0[l - 1], rwkv_v1[l - 1], rwkv_v2[l - 1])
        m, nn, q, y0, bonus, g, v_l = _rwkv_prep(prw, v_first, rp, vres, bsz, t, tt_prep)
        if l == 0:
            v_first = v_l
        yr = _rwkv_scan(m, nn, q, y0, bonus, g, rwkv_gn_g[l], rwkv_gn_b[l], bsz, t, tt_scan)

        i = l // 2
        last = l == depth - 1
        if l % 2 == 0:
            h_mid, hn = _outproj(ypc, yr, h, w_out[l].astype(BF16), norm2_g[l], None, tm, BF16)
            fdim = ffn_w_gate.shape[2]
            tr = _pick(n, 512, 8)
            tf = _pick(fdim, 1408, LANES)
            nblk = n // tr
            h = _ffn(hn, ffn_w_gate[i][None].astype(BF16), ffn_w_up[i][None].astype(BF16),
                     ffn_w_down[i][None].astype(BF16), jnp.zeros((nblk,), jnp.int32),
                     jnp.full((1,), nblk, jnp.int32), h_mid, tr, tf)
            if last:
                h = _rmsnorm(h, final_g, tm)
        else:
            n_exp = moe_router.shape[2]
            h_mid, hn, route, cnt = _outproj(ypc, yr, h, w_out[l].astype(BF16), norm2_g[l],
                                             moe_router[i], tm, F32)
            gates = route
            top_e = route[:, 2:2 + N_TOP].astype(jnp.int32)
            rank = route[:, 4:4 + N_TOP].astype(jnp.int32)
            counts = cnt[0, :n_exp].astype(jnp.int32)
            tr = _pick(n, 512, 8)
            padded = (counts + tr - 1) // tr * tr
            pad_ends = jnp.cumsum(padded)
            pad_starts = pad_ends - padded
            dest = pad_starts[top_e] + rank
            nblk = (n * N_TOP) // tr + n_exp
            block_e = jnp.minimum(
                jnp.searchsorted(pad_ends, jnp.arange(nblk, dtype=jnp.int32) * tr, side="right"),
                n_exp - 1).astype(jnp.int32)
            n_used = (pad_ends[-1] // tr).astype(jnp.int32).reshape(1)
            xd = _dispatch(hn, dest, nblk * tr, tm_moe)
            fdim = moe_w_gate.shape[3]
            tf = _pick(fdim, 896, LANES)
            yd = _ffn(xd, moe_w_gate[i].astype(BF16), moe_w_up[i].astype(BF16),
                      moe_w_down[i].astype(BF16), block_e, n_used, None, tr, tf)
            h = _combine(yd, dest, gates, h_mid, final_g, tm_moe, last)
    return h.reshape(bsz, t, d)
```

```python
import functools

import jax
import jax.numpy as jnp
from jax import lax
from jax.experimental import pallas as pl
from jax.experimental.pallas import tpu as pltpu

F32 = jnp.float32
BF16 = jnp.bfloat16

HEAD = 64
CHUNK = 64
POOL_WINDOWS = (2, 4, 8, 16)
CONV_KERNEL = 31
HALO = 32
RMS_EPS = 1e-6
LN_EPS = 1e-5
GN_EPS = 64e-5
N_TOP = 2
LANES = 128
VMEM_LIMIT = 56 * 1024 * 1024
SCAN_TERMS = 1
PREP_CHUNKS_PER_STEP = 4
DMA_UNROLL = 8

NN = ((1,), (0,))
NT = ((1,), (1,))
TN = ((0,), (0,))


def _split(x, n):
    if x.dtype == BF16:
        return [x]
    parts = []
    rem = x
    for i in range(n):
        p = rem.astype(BF16)
        parts.append(p)
        if i + 1 < n:
            rem = rem - p.astype(F32)
    return parts


def _mm(a, b, dims=NN, na=1, nb=1):
    pa = _split(a, na)
    pb = _split(b, nb)
    nmax = max(len(pa), len(pb))
    out = None
    for i, x in enumerate(pa):
        for j, y in enumerate(pb):
            if i + j < nmax:
                t = lax.dot_general(x, y, (dims, ((), ())), preferred_element_type=F32)
                out = t if out is None else out + t
    return out


def _sigmoid(x):
    return 1.0 / (1.0 + jnp.exp(-x))


def _silu(x):
    return x * _sigmoid(x)


def _cparams(sem):
    return pltpu.CompilerParams(dimension_semantics=sem, vmem_limit_bytes=VMEM_LIMIT)


def _inproj_kernel(x_ref, g_ref, w_ref, op_ref, oc_ref, or_ref, *, n_pool, n_conv):
    x = x_ref[...]
    ms = jnp.mean(x * x, axis=-1, keepdims=True)
    y = x * lax.rsqrt(ms + RMS_EPS) * g_ref[...]
    p = jnp.dot(y.astype(BF16), w_ref[...], preferred_element_type=F32)
    op_ref[...] = p[:, :n_pool]
    oc_ref[...] = p[:, n_pool:n_pool + n_conv]
    or_ref[...] = p[:, n_pool + n_conv:]


def _inproj(h, g, w_bf16, n_pool, n_conv, tm):
    n, d = h.shape
    n_in = w_bf16.shape[1]
    n_rw = n_in - n_pool - n_conv
    return pl.pallas_call(
        functools.partial(_inproj_kernel, n_pool=n_pool, n_conv=n_conv),
        grid=(n // tm,),
        in_specs=[
            pl.BlockSpec((tm, d), lambda i: (i, 0)),
            pl.BlockSpec((1, d), lambda i: (0, 0)),
            pl.BlockSpec((d, n_in), lambda i: (0, 0)),
        ],
        out_specs=[
            pl.BlockSpec((tm, n_pool), lambda i: (i, 0)),
            pl.BlockSpec((tm, n_conv), lambda i: (i, 0)),
            pl.BlockSpec((tm, n_rw), lambda i: (i, 0)),
        ],
        out_shape=[
            jax.ShapeDtypeStruct((n, n_pool), F32),
            jax.ShapeDtypeStruct((n, n_conv), F32),
            jax.ShapeDtypeStruct((n, n_rw), F32),
        ],
        compiler_params=_cparams(("arbitrary",)),
        name="inproj",
    )(h, g.reshape(1, d), w_bf16)


def _poolconv_kernel(pp_ref, pc_ref, pw_ref, ps_ref, wl_ref, cw_ref, cb_ref, lg_ref, lb_ref,
                     o_ref, extp, extc, *, tt, cp, cc):
    j = pl.program_id(1)

    @pl.when(j == 0)
    def _():
        extp[0:HALO, :] = jnp.zeros((HALO, cp), F32)
        extc[0:HALO, :] = jnp.zeros((HALO, cc), F32)

    u = pp_ref[...]
    extp[HALO:HALO + tt, :] = u
    wl = wl_ref[...]
    run = u
    sel = jnp.zeros_like(u)
    k = 1
    for win in POOL_WINDOWS:
        while k < win:
            run = run + extp[pl.ds(HALO - k, tt), :]
            k += 1
        sel = jnp.where(wl == float(win), run, sel)
    pos = (j * tt + lax.broadcasted_iota(jnp.int32, (tt, 1), 0) + 1).astype(F32)
    d = sel / jnp.minimum(pos, wl) - u
    yp = _mm(d, pw_ref[...], NN, 2, 2) * ps_ref[...]
    extp[0:HALO, :] = extp[tt:tt + HALO, :]

    pc = pc_ref[...]
    uc = pc[:, :cc] * _sigmoid(pc[:, cc:])
    extc[HALO:HALO + tt, :] = uc
    acc = jnp.zeros((tt, cc), F32)
    for t in range(CONV_KERNEL):
        acc = acc + cw_ref[t:t + 1, :] * extc[pl.ds(HALO - (CONV_KERNEL - 1) + t, tt), :]
    extc[0:HALO, :] = extc[tt:tt + HALO, :]
    yf = acc + cb_ref[...]
    mu = jnp.mean(yf, axis=-1, keepdims=True)
    yc = yf - mu
    var = jnp.mean(yc * yc, axis=-1, keepdims=True)
    yn = yc * lax.rsqrt(var + LN_EPS) * lg_ref[...] + lb_ref[...]
    o_ref[:, :cp] = yp
    o_ref[:, cp:] = _silu(yn)


def _poolconv(pp, pc, pool_w, pool_scale, conv_w, conv_b, ln_g, ln_b, bsz, t, tt):
    n, cp = pp.shape
    cc = pc.shape[1] // 2
    ng, cg, _ = pool_w.shape
    pw = jnp.zeros((cp, cp), F32)
    for gi in range(ng):
        pw = pw.at[gi * cg:(gi + 1) * cg, gi * cg:(gi + 1) * cg].set(pool_w[gi])
    wl = jnp.repeat(jnp.asarray(POOL_WINDOWS, F32), cg).reshape(1, cp)
    nt = t // tt
    row = lambda b, j: (b * nt + j, 0)
    const = lambda b, j: (0, 0)
    return pl.pallas_call(
        functools.partial(_poolconv_kernel, tt=tt, cp=cp, cc=cc),
        grid=(bsz, nt),
        in_specs=[
            pl.BlockSpec((tt, cp), row),
            pl.BlockSpec((tt, 2 * cc), row),
            pl.BlockSpec((cp, cp), const),
            pl.BlockSpec((1, cp), const),
            pl.BlockSpec((1, cp), const),
            pl.BlockSpec((CONV_KERNEL, cc), const),
            pl.BlockSpec((1, cc), const),
            pl.BlockSpec((1, cc), const),
            pl.BlockSpec((1, cc), const),
        ],
        out_specs=pl.BlockSpec((tt, cp + cc), row),
        out_shape=jax.ShapeDtypeStruct((n, cp + cc), F32),
        scratch_shapes=[pltpu.VMEM((tt + HALO, cp), F32), pltpu.VMEM((tt + HALO, cc), F32)],
        compiler_params=_cparams(("arbitrary", "arbitrary")),
        name="poolconv",
    )(pp, pc, pw, pool_scale.reshape(1, cp), wl, conv_w, conv_b.reshape(1, cc),
      ln_g.reshape(1, cc), ln_b.reshape(1, cc))


def _rwkv_prep_kernel(*refs, tt, wd, r_dec, r_icl, r_gate, has_vres):
    it = iter(refs)
    c_ref = next(it)
    vf_ref = next(it) if has_vres else None
    mu_ref, w0_ref, w2_ref, a0_ref, a2_ref, g2_ref, kk_ref, ka_ref, rk_ref = (next(it) for _ in range(9))
    if has_vres:
        v0_ref, v1_ref, v2_ref = (next(it) for _ in range(3))
    ones_ref, tril_ref, blk_ref = next(it), next(it), next(it)
    m_ref, n_ref, q_ref, y0_ref, bonus_ref, g_ref, v_ref = (next(it) for _ in range(7))
    ext = next(it)
    s_at, s_rt, s_bt, s_kt, s_btp, s_ktp, s_pc, s_v = (next(it) for _ in range(8))

    j = pl.program_id(1)
    ncols = c_ref.shape[1]

    @pl.when(j == 0)
    def _():
        ext[0:8, :] = jnp.zeros((8, ncols), F32)

    c = c_ref[...]
    ext[8:8 + tt, :] = c
    prev = ext[pl.ds(7, tt), :]
    ext[0:8, :] = ext[tt:tt + 8, :]
    cf = c + mu_ref[...] * (prev - c)

    r = cf[:, :wd]
    k = cf[:, wd:2 * wd]
    v = cf[:, 2 * wd:3 * wd]
    o = 3 * wd
    xw = cf[:, o:o + r_dec]
    o += r_dec
    xa = cf[:, o:o + r_icl]
    o += r_icl
    xg = cf[:, o:o + r_gate]

    z = w0_ref[...] + _mm(jnp.tanh(xw), w2_ref[...], NN, 2, 2)
    sp = jnp.maximum(-z, 0.0) + jnp.log1p(jnp.exp(-jnp.abs(z)))
    w_log = -sp - 0.5
    lw = -jnp.exp(w_log)
    a = _sigmoid(a0_ref[...] + _mm(xa, a2_ref[...], NN, 2, 2))
    g = _mm(_sigmoid(xg), g2_ref[...], NN, 2, 2)
    if has_vres:
        lo = _mm(_mm(v, v1_ref[...], NN, 2, 2), v2_ref[...], NN, 2, 2)
        v = v + (vf_ref[...] - v) * _sigmoid(v0_ref[...] + lo)

    ones = ones_ref[...]
    kkp = k * kk_ref[...]
    nrm = jnp.sqrt(_mm(kkp * kkp, ones, NN, 2, 1))
    kk = kkp / jnp.maximum(nrm, 1e-12)
    k2 = k * (1.0 + (a - 1.0) * ka_ref[...])
    bonus_ref[...] = _mm(r * k2 * rk_ref[...], ones, NN, 2, 1) * v
    g_ref[...] = g
    v_ref[...] = v

    cl = _mm(tril_ref[...], lw, NN, 1, 3)
    lc = _mm(blk_ref[...], lw, NN, 1, 3)
    e_neg = jnp.exp(-cl)
    e_rem = jnp.exp(lc - cl)
    s_at[...] = -kk * jnp.exp(cl - lw)
    s_rt[...] = r * jnp.exp(cl)
    s_bt[...] = kk * a * e_neg
    s_kt[...] = k2 * e_neg
    s_btp[...] = kk * a * e_rem
    s_ktp[...] = k2 * e_rem
    s_pc[...] = jnp.exp(lc)
    s_v[...] = v

    ri = lax.broadcasted_iota(jnp.int32, (CHUNK, CHUNK), 0)
    ci = lax.broadcasted_iota(jnp.int32, (CHUNK, CHUNK), 1)
    strict = ri > ci
    incl = ri >= ci
    eye = (ri == ci).astype(F32)

    def chunk_body(it, carry):
        units = []
        for cc in range(PREP_CHUNKS_PER_STEP):
            rs = pl.ds(pl.multiple_of((it * PREP_CHUNKS_PER_STEP + cc) * CHUNK, CHUNK), CHUNK)
            units += [(rs, slice(h * HEAD, (h + 1) * HEAD)) for h in range(wd // HEAD)]
        nu = len(units)
        sp = SCAN_TERMS
        at = [s_at[u] for u in units]
        rt = [s_rt[u] for u in units]
        vv = [s_v[u] for u in units]
        aa = [_mm(jnp.concatenate([at[i], rt[i]], axis=0),
                  jnp.concatenate([s_bt[u], s_kt[u]], axis=0), NT, sp, sp)
              for i, u in enumerate(units)]
        p = [jnp.where(strict, a[:CHUNK, :CHUNK], 0.0) for a in aa]
        a_ak = [jnp.where(strict, a[:CHUNK, CHUNK:], 0.0) for a in aa]
        a_rb = [jnp.where(incl, a[CHUNK:, :CHUNK], 0.0) for a in aa]
        a_rk = [jnp.where(incl, a[CHUNK:, CHUNK:], 0.0) for a in aa]
        x = [jnp.concatenate([at[i], _mm(a_ak[i], vv[i], NN, sp, sp)], axis=1) for i in range(nu)]
        nsteps = CHUNK.bit_length() - 1
        for s in range(nsteps):
            x = [xi + _mm(pi, xi, NN, sp, sp) for pi, xi in zip(p, x)]
            if s + 1 < nsteps:
                p = [_mm(pi, pi, NN, sp, sp) for pi in p]
        gt = [_mm(s_btp[u], x[i], TN, sp, sp) for i, u in enumerate(units)]
        gb = [_mm(a_rb[i], x[i], NN, sp, sp) for i in range(nu)]
        kv = [_mm(s_ktp[u], vv[i], TN, sp, sp) for i, u in enumerate(units)]
        yv = [_mm(a_rk[i], vv[i], NN, sp, sp) for i in range(nu)]
        for i, u in enumerate(units):
            m_ref[u] = eye * s_pc[u] + gt[i][:, :HEAD]
            n_ref[u] = gt[i][:, HEAD:] + kv[i]
            q_ref[u] = rt[i] + gb[i][:, :HEAD]
            y0_ref[u] = gb[i][:, HEAD:] + yv[i]
        return carry

    lax.fori_loop(0, tt // (CHUNK * PREP_CHUNKS_PER_STEP), chunk_body, 0)


def _rwkv_prep(c, v_first, p, vres, bsz, t, tt):
    n, ncols = c.shape
    wd = p["w0"].shape[0]
    r_dec, r_icl, r_gate = p["w2"].shape[0], p["a2"].shape[0], p["g2"].shape[0]
    has_vres = vres is not None
    nt = t // tt
    row = lambda b, j: (b * nt + j, 0)
    const = lambda b, j: (0, 0)
    vec = lambda a: a.reshape(1, -1)
    hid = jnp.arange(wd) // HEAD
    ones = (hid[:, None] == hid[None, :]).astype(BF16)
    ti = jnp.arange(tt)
    same = ti[:, None] // CHUNK == ti[None, :] // CHUNK
    tril = (same & (ti[:, None] >= ti[None, :])).astype(BF16)
    blk = same.astype(BF16)

    args = [c]
    specs = [pl.BlockSpec((tt, ncols), row)]
    if has_vres:
        args.append(v_first)
        specs.append(pl.BlockSpec((tt, wd), row))
    small = [vec(p["mu"]), vec(p["w0"]), p["w2"], vec(p["a0"]), p["a2"], p["g2"],
             vec(p["k_k"]), vec(p["k_a"]), vec(p["r_k"])]
    if has_vres:
        small += [vec(vres[0]), vres[1], vres[2]]
    small += [ones, tril, blk]
    for s in small:
        args.append(s)
        specs.append(pl.BlockSpec(s.shape, const))
    out_spec = pl.BlockSpec((tt, wd), row)
    out_sds = jax.ShapeDtypeStruct((n, wd), F32)
    return pl.pallas_call(
        functools.partial(_rwkv_prep_kernel, tt=tt, wd=wd, r_dec=r_dec, r_icl=r_icl,
                          r_gate=r_gate, has_vres=has_vres),
        grid=(bsz, nt),
        in_specs=specs,
        out_specs=[out_spec] * 7,
        out_shape=[out_sds] * 7,
        scratch_shapes=[pltpu.VMEM((tt + 8, ncols), F32)] + [pltpu.VMEM((tt, wd), F32)] * 8,
        compiler_params=_cparams(("arbitrary", "arbitrary")),
        name="rwkv_prep",
    )(*args)


def _rwkv_scan_kernel(m_ref, n_ref, q_ref, y0_ref, bonus_ref, g_ref, gg_ref, gb_ref, ones_ref,
                      o_ref, z_ref, y_ref, *, bsz, tt, wd):
    j = pl.program_id(0)
    nh = wd // HEAD

    @pl.when(j == 0)
    def _():
        z_ref[...] = jnp.zeros(z_ref.shape, F32)

    for ch in range(tt // CHUNK):
        rs = slice(ch * CHUNK, (ch + 1) * CHUNK)
        for b in range(bsz):
            for h in range(nh):
                hs = slice(h * HEAD, (h + 1) * HEAD)
                z = z_ref[b * nh + h]
                qm = jnp.concatenate([q_ref[b, rs, hs], m_ref[b, rs, hs]], axis=0)
                res = _mm(qm, z, NN, 1, 2)
                y_ref[b, rs, hs] = res[:CHUNK] + y0_ref[b, rs, hs]
                z_ref[b * nh + h] = res[CHUNK:] + n_ref[b, rs, hs]

    ones = ones_ref[...]
    for b in range(bsz):
        y = y_ref[b]
        ym = _mm(y, ones, NN, 2, 1) * (1.0 / HEAD)
        yc = y - ym
        yv = _mm(yc * yc, ones, NN, 2, 1) * (1.0 / HEAD)
        yn = yc * lax.rsqrt(yv + GN_EPS) * gg_ref[...] + gb_ref[...]
        o_ref[b] = (yn + bonus_ref[b]) * g_ref[b]


def _rwkv_scan(m, nn, q, y0, bonus, g, gn_g, gn_b, bsz, t, tt):
    n, wd = m.shape
    nh = wd // HEAD
    hid = jnp.arange(wd) // HEAD
    ones = (hid[:, None] == hid[None, :]).astype(BF16)
    r3 = lambda a: a.reshape(bsz, t, wd)
    blk = pl.BlockSpec((bsz, tt, wd), lambda j: (0, j, 0))
    const = lambda j: (0, 0)
    out = pl.pallas_call(
        functools.partial(_rwkv_scan_kernel, bsz=bsz, tt=tt, wd=wd),
        grid=(t // tt,),
        in_specs=[blk] * 6 + [pl.BlockSpec((1, wd), const), pl.BlockSpec((1, wd), const),
                              pl.BlockSpec((wd, wd), const)],
        out_specs=blk,
        out_shape=jax.ShapeDtypeStruct((bsz, t, wd), F32),
        scratch_shapes=[pltpu.VMEM((bsz * nh, HEAD, HEAD), F32), pltpu.VMEM((bsz, tt, wd), F32)],
        compiler_params=_cparams(("arbitrary",)),
        name="rwkv_scan",
    )(r3(m), r3(nn), r3(q), r3(y0), r3(bonus), r3(g), gn_g.reshape(1, wd), gn_b.reshape(1, wd), ones)
    return out.reshape(n, wd)


def _outproj_kernel(*refs, n_pc, route, n_exp):
    if route:
        (ypc_ref, yr_ref, h_ref, w_ref, g_ref, rt_ref, tri_ref,
         ho_ref, hn_ref, route_ref, cnt_ref, carry) = refs
    else:
        ypc_ref, yr_ref, h_ref, w_ref, g_ref, ho_ref, hn_ref = refs
    mix = (jnp.dot(ypc_ref[...].astype(BF16), w_ref[:n_pc, :], preferred_element_type=F32)
           + jnp.dot(yr_ref[...].astype(BF16), w_ref[n_pc:, :], preferred_element_type=F32))
    h = h_ref[...] + mix
    ho_ref[...] = h
    ms = jnp.mean(h * h, axis=-1, keepdims=True)
    hn = h * lax.rsqrt(ms + RMS_EPS) * g_ref[...]
    hn_ref[...] = hn.astype(hn_ref.dtype)
    if not route:
        return

    i = pl.program_id(0)

    @pl.when(i == 0)
    def _():
        carry[...] = jnp.zeros(carry.shape, F32)

    tm = hn.shape[0]
    logits = _mm(hn, rt_ref[...], NN, 2, 2)
    lane = lax.broadcasted_iota(jnp.int32, (tm, LANES), 1).astype(F32)
    neg = jnp.float32(-jnp.inf)
    logits = jnp.where(lane < n_exp, logits, neg)
    m1 = jnp.max(logits, axis=-1, keepdims=True)
    i1 = jnp.min(jnp.where(logits == m1, lane, float(LANES)), axis=-1, keepdims=True)
    rest = jnp.where(lane == i1, neg, logits)
    m2 = jnp.max(rest, axis=-1, keepdims=True)
    i2 = jnp.min(jnp.where(rest == m2, lane, float(LANES)), axis=-1, keepdims=True)
    e = jnp.exp(m2 - m1)
    g1 = 1.0 / (1.0 + e)
    g2 = e / (1.0 + e)
    oh1 = (lane == i1).astype(F32)
    oh2 = (lane == i2).astype(F32)
    both = oh1 + oh2
    before = carry[0:1, :] + _mm(tri_ref[...], both.astype(BF16), NN, 1, 1)
    rank1 = jnp.sum(before * oh1, axis=-1, keepdims=True)
    rank2 = jnp.sum(before * oh2, axis=-1, keepdims=True)
    tot = carry[0:1, :] + jnp.sum(both, axis=0, keepdims=True)
    carry[...] = jnp.broadcast_to(tot, carry.shape)
    cnt_ref[...] = jnp.broadcast_to(tot, cnt_ref.shape)
    out = jnp.where(lane == 0, g1, 0.0)
    out = jnp.where(lane == 1, g2, out)
    out = jnp.where(lane == 2, i1, out)
    out = jnp.where(lane == 3, i2, out)
    out = jnp.where(lane == 4, rank1, out)
    out = jnp.where(lane == 5, rank2, out)
    route_ref[...] = out


def _outproj(ypc, yr, h, w_bf16, g, router, tm, hn_dtype):
    n, d = h.shape
    n_pc, n_r = ypc.shape[1], yr.shape[1]
    route = router is not None
    rowb = lambda w: pl.BlockSpec((tm, w), lambda i: (i, 0))
    const = lambda i: (0, 0)
    args = [ypc, yr, h, w_bf16, g.reshape(1, d)]
    specs = [rowb(n_pc), rowb(n_r), rowb(d), pl.BlockSpec((d, d), const), pl.BlockSpec((1, d), const)]
    out_specs = [rowb(d), rowb(d)]
    out_shape = [jax.ShapeDtypeStruct((n, d), F32), jax.ShapeDtypeStruct((n, d), hn_dtype)]
    scratch = []
    n_exp = 0
    if route:
        n_exp = router.shape[1]
        rt = jnp.zeros((d, LANES), F32).at[:, :n_exp].set(router)
        ti = jnp.arange(tm)
        tri = (ti[:, None] > ti[None, :]).astype(BF16)
        args += [rt, tri]
        specs += [pl.BlockSpec((d, LANES), const), pl.BlockSpec((tm, tm), const)]
        out_specs += [rowb(LANES), pl.BlockSpec((8, LANES), const)]
        out_shape += [jax.ShapeDtypeStruct((n, LANES), F32), jax.ShapeDtypeStruct((8, LANES), F32)]
        scratch = [pltpu.VMEM((8, LANES), F32)]
    return pl.pallas_call(
        functools.partial(_outproj_kernel, n_pc=n_pc, route=route, n_exp=n_exp),
        grid=(n // tm,),
        in_specs=specs,
        out_specs=out_specs,
        out_shape=out_shape,
        scratch_shapes=scratch,
        compiler_params=_cparams(("arbitrary",)),
        name="outproj_route" if route else "outproj",
    )(*args)


def _ffn_kernel(be_ref, nu_ref, *refs, has_res):
    if has_res:
        x_ref, wg_ref, wu_ref, wd_ref, res_ref, o_ref = refs
    else:
        x_ref, wg_ref, wu_ref, wd_ref, o_ref = refs
    del be_ref
    used = pl.program_id(0) < nu_ref[0]

    @pl.when(used)
    def _():
        x = x_ref[...].astype(BF16)
        gate = jnp.dot(x, wg_ref[0], preferred_element_type=F32)
        up = jnp.dot(x, wu_ref[0], preferred_element_type=F32)
        mid = (_silu(gate) * up).astype(BF16)
        y = jnp.dot(mid, wd_ref[0], preferred_element_type=F32)
        o_ref[...] = y + res_ref[...] if has_res else y

    @pl.when(jnp.logical_not(used))
    def _():
        o_ref[...] = res_ref[...] if has_res else jnp.zeros(o_ref.shape, F32)


def _ffn(x, wg, wu, wd, block_e, n_used, res, tr):
    rows, d = x.shape
    _, _, fdim = wg.shape
    has_res = res is not None
    resident = pl.Buffered(1)
    args = [x, wg, wu, wd]
    specs = [
        pl.BlockSpec((tr, d), lambda i, be, nu: (i, 0)),
        pl.BlockSpec((1, d, fdim), lambda i, be, nu: (be[i], 0, 0), pipeline_mode=resident),
        pl.BlockSpec((1, d, fdim), lambda i, be, nu: (be[i], 0, 0), pipeline_mode=resident),
        pl.BlockSpec((1, fdim, d), lambda i, be, nu: (be[i], 0, 0), pipeline_mode=resident),
    ]
    if has_res:
        args.append(res)
        specs.append(pl.BlockSpec((tr, d), lambda i, be, nu: (i, 0)))
    return pl.pallas_call(
        functools.partial(_ffn_kernel, has_res=has_res),
        grid_spec=pltpu.PrefetchScalarGridSpec(
            num_scalar_prefetch=2,
            grid=(rows // tr,),
            in_specs=specs,
            out_specs=pl.BlockSpec((tr, d), lambda i, be, nu: (i, 0)),
        ),
        out_shape=jax.ShapeDtypeStruct((rows, d), F32),
        compiler_params=_cparams(("arbitrary",)),
        name="ffn_res" if has_res else "ffn_grouped",
    )(block_e, n_used, *args)


def _row_copy(src_ref, s, dst_ref, d, sem):
    return pltpu.make_async_copy(src_ref.at[pl.ds(s, 1), :], dst_ref.at[pl.ds(d, 1), :], sem)


def _dispatch_kernel(dest_ref, x_ref, xd_in_ref, xd_ref, sem, *, tm):
    del xd_in_ref

    def issue(r, carry):
        for kk in range(N_TOP):
            _row_copy(x_ref, r, xd_ref, dest_ref[0, 0, N_TOP * r + kk], sem).start()
        return carry

    lax.fori_loop(0, tm, issue, 0, unroll=DMA_UNROLL)

    def drain(r, carry):
        for kk in range(N_TOP):
            _row_copy(x_ref, 0, xd_ref, 0, sem).wait()
        return carry

    lax.fori_loop(0, tm, drain, 0, unroll=DMA_UNROLL)


def _dispatch(x, dest, n_rows, tm):
    n, d = x.shape
    nblk = n // tm
    dest3 = dest.reshape(nblk, 1, N_TOP * tm)
    xd0 = jnp.zeros((n_rows, d), x.dtype)
    return pl.pallas_call(
        functools.partial(_dispatch_kernel, tm=tm),
        grid=(nblk,),
        in_specs=[
            pl.BlockSpec((1, 1, N_TOP * tm), lambda i: (i, 0, 0), memory_space=pltpu.SMEM),
            pl.BlockSpec((tm, d), lambda i: (i, 0)),
            pl.BlockSpec(memory_space=pl.ANY),
        ],
        out_specs=pl.BlockSpec(memory_space=pl.ANY),
        out_shape=jax.ShapeDtypeStruct((n_rows, d), x.dtype),
        scratch_shapes=[pltpu.SemaphoreType.DMA(())],
        input_output_aliases={2: 0},
        compiler_params=_cparams(("arbitrary",)),
        name="moe_dispatch",
    )(dest3, x, xd0)


def _combine_kernel(dest_ref, yd_ref, gate_ref, h_ref, fg_ref, o_ref, buf, sem, *, tm, final_norm):
    def issue(r, carry):
        for kk in range(N_TOP):
            _row_copy(yd_ref, dest_ref[0, 0, N_TOP * r + kk], buf.at[kk], r, sem).start()
        return carry

    lax.fori_loop(0, tm, issue, 0, unroll=DMA_UNROLL)

    def drain(r, carry):
        for kk in range(N_TOP):
            _row_copy(yd_ref, 0, buf.at[kk], 0, sem).wait()
        return carry

    lax.fori_loop(0, tm, drain, 0, unroll=DMA_UNROLL)

    gate = gate_ref[...]
    h = h_ref[...]
    for kk in range(N_TOP):
        h = h + buf[kk] * gate[:, kk:kk + 1]
    if final_norm:
        ms = jnp.mean(h * h, axis=-1, keepdims=True)
        h = h * lax.rsqrt(ms + RMS_EPS) * fg_ref[...]
    o_ref[...] = h


def _combine(yd, dest, gates, h, final_g, tm, final_norm):
    n, d = h.shape
    nblk = n // tm
    dest3 = dest.reshape(nblk, 1, N_TOP * tm)
    return pl.pallas_call(
        functools.partial(_combine_kernel, tm=tm, final_norm=final_norm),
        grid=(nblk,),
        in_specs=[
            pl.BlockSpec((1, 1, N_TOP * tm), lambda i: (i, 0, 0), memory_space=pltpu.SMEM),
            pl.BlockSpec(memory_space=pl.ANY),
            pl.BlockSpec((tm, LANES), lambda i: (i, 0)),
            pl.BlockSpec((tm, d), lambda i: (i, 0)),
            pl.BlockSpec((1, d), lambda i: (0, 0)),
        ],
        out_specs=pl.BlockSpec((tm, d), lambda i: (i, 0)),
        out_shape=jax.ShapeDtypeStruct((n, d), F32),
        scratch_shapes=[pltpu.VMEM((N_TOP, tm, d), F32), pltpu.SemaphoreType.DMA(())],
        compiler_params=_cparams(("arbitrary",)),
        name="moe_combine",
    )(dest3, yd, gates, h, final_g.reshape(1, d))


def _rmsnorm_kernel(x_ref, g_ref, o_ref):
    x = x_ref[...]
    ms = jnp.mean(x * x, axis=-1, keepdims=True)
    o_ref[...] = x * lax.rsqrt(ms + RMS_EPS) * g_ref[...]


def _rmsnorm(x, g, tm):
    n, d = x.shape
    return pl.pallas_call(
        _rmsnorm_kernel,
        grid=(n // tm,),
        in_specs=[pl.BlockSpec((tm, d), lambda i: (i, 0)), pl.BlockSpec((1, d), lambda i: (0, 0))],
        out_specs=pl.BlockSpec((tm, d), lambda i: (i, 0)),
        out_shape=jax.ShapeDtypeStruct((n, d), F32),
        compiler_params=_cparams(("arbitrary",)),
        name="final_rmsnorm",
    )(x, g.reshape(1, d))


def _pick(total, target, mult):
    best = None
    for cand in range(mult, min(total, target) + 1, mult):
        if total % cand == 0:
            best = cand
    assert best is not None, (total, target, mult)
    return best


def kernel(x, norm1_g, w_in, pool_w, pool_scale, conv_w, conv_b, conv_ln_g, conv_ln_b, shift_mu, rwkv_w0, rwkv_w2, rwkv_a0, rwkv_a2, rwkv_g2, rwkv_k_k, rwkv_k_a, rwkv_r_k, rwkv_gn_g, rwkv_gn_b, rwkv_v0, rwkv_v1, rwkv_v2, w_out, norm2_g, ffn_w_gate, ffn_w_up, ffn_w_down, moe_router, moe_w_gate, moe_w_up, moe_w_down, final_g):
    bsz, t, d = x.shape
    depth = w_in.shape[0]
    n = bsz * t
    n_pool = pool_scale.shape[1]
    n_conv = 2 * conv_b.shape[1]
    assert t % CHUNK == 0

    tm = _pick(n, 512, 8)
    tt_pc = _pick(t, 512, HALO)
    tt_prep = _pick(t, 256, CHUNK)
    tt_scan = _pick(t, 128, CHUNK)
    tm_moe = _pick(n, 256, 8)

    h = x.reshape(n, d)
    v_first = None
    for l in range(depth):
        pp, pc, prw = _inproj(h, norm1_g[l], w_in[l].astype(BF16), n_pool, n_conv, tm)
        ypc = _poolconv(pp, pc, pool_w[l], pool_scale[l], conv_w[l], conv_b[l],
                        conv_ln_g[l], conv_ln_b[l], bsz, t, tt_pc)
        rp = dict(mu=shift_mu[l], w0=rwkv_w0[l], w2=rwkv_w2[l], a0=rwkv_a0[l], a2=rwkv_a2[l],
                  g2=rwkv_g2[l], k_k=rwkv_k_k[l], k_a=rwkv_k_a[l], r_k=rwkv_r_k[l])
        vres = None if l == 0 else (rwkv_v0[l - 1], rwkv_v1[l - 1], rwkv_v2[l - 1])
        m, nn, q, y0, bonus, g, v_l = _rwkv_prep(prw, v_first, rp, vres, bsz, t, tt_prep)
        if l == 0:
            v_first = v_l
        yr = _rwkv_scan(m, nn, q, y0, bonus, g, rwkv_gn_g[l], rwkv_gn_b[l], bsz, t, tt_scan)

        i = l // 2
        last = l == depth - 1
        if l % 2 == 0:
            h_mid, hn = _outproj(ypc, yr, h, w_out[l].astype(BF16), norm2_g[l], None, tm, BF16)
            tr = _pick(n, 512, 8)
            nblk = n // tr
            h = _ffn(hn, ffn_w_gate[i][None].astype(BF16), ffn_w_up[i][None].astype(BF16),
                     ffn_w_down[i][None].astype(BF16), jnp.zeros((nblk,), jnp.int32),
                     jnp.full((1,), nblk, jnp.int32), h_mid, tr)
            if last:
                h = _rmsnorm(h, final_g, tm)
        else:
            n_exp = moe_router.shape[2]
            h_mid, hn, route, cnt = _outproj(ypc, yr, h, w_out[l].astype(BF16), norm2_g[l],
                                             moe_router[i], tm, F32)
            gates = route
            top_e = route[:, 2:2 + N_TOP].astype(jnp.int32)
            rank = route[:, 4:4 + N_TOP].astype(jnp.int32)
            counts = cnt[0, :n_exp].astype(jnp.int32)
            tr = _pick(n, 512, 8)
            padded = (counts + tr - 1) // tr * tr
            pad_ends = jnp.cumsum(padded)
            pad_starts = pad_ends - padded
            dest = pad_starts[top_e] + rank
            nblk = (n * N_TOP) // tr + n_exp
            block_e = jnp.minimum(
                jnp.searchsorted(pad_ends, jnp.arange(nblk, dtype=jnp.int32) * tr, side="right"),
                n_exp - 1).astype(jnp.int32)
            n_used = (pad_ends[-1] // tr).astype(jnp.int32).reshape(1)
            xd = _dispatch(hn, dest, nblk * tr, tm_moe)
            yd = _ffn(xd, moe_w_gate[i].astype(BF16), moe_w_up[i].astype(BF16),
                      moe_w_down[i].astype(BF16), block_e, n_used, None, tr)
            h = _combine(yd, dest, gates, h_mid, final_g, tm_moe, last)
    return h.reshape(bsz, t, d)
```

```python
import functools

import jax
import jax.numpy as jnp
from jax import lax
from jax.experimental import pallas as pl
from jax.experimental.pallas import tpu as pltpu

F32 = jnp.float32
BF16 = jnp.bfloat16

HEAD = 64
CHUNK = 64
POOL_WINDOWS = (2, 4, 8, 16)
CONV_KERNEL = 31
HALO = 32
RMS_EPS = 1e-6
LN_EPS = 1e-5
GN_EPS = 64e-5
DECAY_SCALE = 0.6065306597126334
N_TOP = 2
LANES = 128
SUBLANES = 8
VMEM_LIMIT = 56 * 1024 * 1024
SCAN_TERMS = 1
PREP_CHUNKS_PER_STEP = 4
DMA_UNROLL = 8

NN = ((1,), (0,))
NT = ((1,), (1,))
TN = ((0,), (0,))


def _split(x, n):
    if x.dtype == BF16:
        return [x]
    parts = []
    rem = x
    for i in range(n):
        p = rem.astype(BF16)
        parts.append(p)
        if i + 1 < n:
            rem = rem - p.astype(F32)
    return parts


def _mm(a, b, dims=NN, na=1, nb=1):
    pa = _split(a, na)
    pb = _split(b, nb)
    nmax = max(len(pa), len(pb))
    out = None
    for i, x in enumerate(pa):
        for j, y in enumerate(pb):
            if i + j < nmax:
                t = lax.dot_general(x, y, (dims, ((), ())), preferred_element_type=F32)
                out = t if out is None else out + t
    return out


def _sigmoid(x):
    return 0.5 * jnp.tanh(0.5 * x) + 0.5


def _silu(x):
    return x * _sigmoid(x)


def _cparams(sem):
    return pltpu.CompilerParams(dimension_semantics=sem, vmem_limit_bytes=VMEM_LIMIT)


def _inproj_kernel(x_ref, g_ref, w_ref, op_ref, oc_ref, or_ref, *, n_pool, n_conv):
    x = x_ref[...]
    ms = jnp.mean(x * x, axis=-1, keepdims=True)
    y = x * lax.rsqrt(ms + RMS_EPS) * g_ref[...]
    p = jnp.dot(y.astype(BF16), w_ref[...], preferred_element_type=F32)
    op_ref[...] = p[:, :n_pool]
    oc_ref[...] = p[:, n_pool:n_pool + n_conv]
    or_ref[...] = p[:, n_pool + n_conv:]


def _inproj(h, g, w_bf16, n_pool, n_conv, tm):
    n, d = h.shape
    n_in = w_bf16.shape[1]
    n_rw = n_in - n_pool - n_conv
    return pl.pallas_call(
        functools.partial(_inproj_kernel, n_pool=n_pool, n_conv=n_conv),
        grid=(n // tm,),
        in_specs=[
            pl.BlockSpec((tm, d), lambda i: (i, 0)),
            pl.BlockSpec((1, d), lambda i: (0, 0)),
            pl.BlockSpec((d, n_in), lambda i: (0, 0)),
        ],
        out_specs=[
            pl.BlockSpec((tm, n_pool), lambda i: (i, 0)),
            pl.BlockSpec((tm, n_conv), lambda i: (i, 0)),
            pl.BlockSpec((tm, n_rw), lambda i: (i, 0)),
        ],
        out_shape=[
            jax.ShapeDtypeStruct((n, n_pool), F32),
            jax.ShapeDtypeStruct((n, n_conv), F32),
            jax.ShapeDtypeStruct((n, n_rw), F32),
        ],
        compiler_params=_cparams(("arbitrary",)),
        name="inproj",
    )(h, g.reshape(1, d), w_bf16)


def _poolconv_kernel(pp_ref, pc_ref, pw_ref, ps_ref, wl_ref, cw_ref, cb_ref, lg_ref, lb_ref,
                     o_ref, extp, extc, psum, rot, *, tt, cp, cc):
    j = pl.program_id(1)
    lext = HALO + tt

    @pl.when(j == 0)
    def _():
        extp[0:HALO, :] = jnp.zeros((HALO, cp), F32)
        extc[0:HALO, :] = jnp.zeros((HALO, cc), F32)

    u = pp_ref[...]
    extp[HALO:HALO + tt, :] = u
    wl = wl_ref[...]
    sel = jnp.zeros_like(u)
    cur, w, lo = extp, 1, 0
    for idx, win in enumerate(POOL_WINDOWS):
        assert win == 2 * w
        lo = -(-(lo + w) // SUBLANES) * SUBLANES
        assert lo <= HALO
        val = cur[lo:lext, :] + cur[pl.ds(lo - w, lext - lo), :]
        if idx + 1 < len(POOL_WINDOWS):
            psum[idx, lo:lext, :] = val
            cur = psum.at[idx]
        sel = jnp.where(wl == float(win), val[HALO - lo:], sel)
        w = win
    pos = (j * tt + lax.broadcasted_iota(jnp.int32, (tt, 1), 0) + 1).astype(F32)
    inv = jnp.where(pos >= wl, 1.0 / wl, 1.0 / pos)
    d = sel * inv - u
    yp = _mm(d, pw_ref[...], NN, 2, 2) * ps_ref[...]
    extp[0:HALO, :] = extp[tt:tt + HALO, :]

    pc = pc_ref[...]
    uc = pc[:, :cc] * _sigmoid(pc[:, cc:])
    extc[HALO:HALO + tt, :] = uc
    nrot = lext - SUBLANES
    for r in range(1, SUBLANES):
        rot[r - 1] = extc[pl.ds(r, nrot), :]
    acc = jnp.zeros((tt, cc), F32)
    for t in range(CONV_KERNEL):
        q, r = divmod(HALO - (CONV_KERNEL - 1) + t, SUBLANES)
        if r == 0:
            tap = extc[pl.ds(SUBLANES * q, tt), :]
        else:
            tap = rot[r - 1, pl.ds(SUBLANES * q, tt), :]
        acc = acc + cw_ref[t:t + 1, :] * tap
    extc[0:HALO, :] = extc[tt:tt + HALO, :]
    yf = acc + cb_ref[...]
    mu = jnp.mean(yf, axis=-1, keepdims=True)
    yc = yf - mu
    var = jnp.mean(yc * yc, axis=-1, keepdims=True)
    yn = yc * lax.rsqrt(var + LN_EPS) * lg_ref[...] + lb_ref[...]
    o_ref[:, :cp] = yp
    o_ref[:, cp:] = _silu(yn)


def _poolconv(pp, pc, pool_w, pool_scale, conv_w, conv_b, ln_g, ln_b, bsz, t, tt):
    n, cp = pp.shape
    cc = pc.shape[1] // 2
    ng, cg, _ = pool_w.shape
    pw = jnp.zeros((cp, cp), F32)
    for gi in range(ng):
        pw = pw.at[gi * cg:(gi + 1) * cg, gi * cg:(gi + 1) * cg].set(pool_w[gi])
    wl = jnp.repeat(jnp.asarray(POOL_WINDOWS, F32), cg).reshape(1, cp)
    nt = t // tt
    row = lambda b, j: (b * nt + j, 0)
    const = lambda b, j: (0, 0)
    return pl.pallas_call(
        functools.partial(_poolconv_kernel, tt=tt, cp=cp, cc=cc),
        grid=(bsz, nt),
        in_specs=[
            pl.BlockSpec((tt, cp), row),
            pl.BlockSpec((tt, 2 * cc), row),
            pl.BlockSpec((cp, cp), const),
            pl.BlockSpec((1, cp), const),
            pl.BlockSpec((1, cp), const),
            pl.BlockSpec((CONV_KERNEL, cc), const),
            pl.BlockSpec((1, cc), const),
            pl.BlockSpec((1, cc), const),
            pl.BlockSpec((1, cc), const),
        ],
        out_specs=pl.BlockSpec((tt, cp + cc), row),
        out_shape=jax.ShapeDtypeStruct((n, cp + cc), F32),
        scratch_shapes=[pltpu.VMEM((tt + HALO, cp), F32), pltpu.VMEM((tt + HALO, cc), F32),
                        pltpu.VMEM((len(POOL_WINDOWS) - 1, tt + HALO, cp), F32),
                        pltpu.VMEM((SUBLANES - 1, tt + HALO - SUBLANES, cc), F32)],
        compiler_params=_cparams(("arbitrary", "arbitrary")),
        name="poolconv",
    )(pp, pc, pw, pool_scale.reshape(1, cp), wl, conv_w, conv_b.reshape(1, cc),
      ln_g.reshape(1, cc), ln_b.reshape(1, cc))


def _rwkv_prep_kernel(*refs, tt, wd, r_dec, r_icl, r_gate, has_vres):
    it = iter(refs)
    c_ref = next(it)
    vf_ref = next(it) if has_vres else None
    mu_ref, w0_ref, w2_ref, a0_ref, a2_ref, g2_ref, kk_ref, ka_ref, rk_ref = (next(it) for _ in range(9))
    if has_vres:
        v0_ref, v1_ref, v2_ref = (next(it) for _ in range(3))
    ones_ref, tril_ref, blk_ref = next(it), next(it), next(it)
    m_ref, n_ref, q_ref, y0_ref, bonus_ref, g_ref, v_ref = (next(it) for _ in range(7))
    ext = next(it)
    s_at, s_rt, s_bt, s_kt, s_btp, s_ktp, s_pc, s_v = (next(it) for _ in range(8))

    j = pl.program_id(1)
    ncols = c_ref.shape[1]

    @pl.when(j == 0)
    def _():
        ext[0:8, :] = jnp.zeros((8, ncols), F32)

    c = c_ref[...]
    ext[8:8 + tt, :] = c
    prev = ext[pl.ds(7, tt), :]
    ext[0:8, :] = ext[tt:tt + 8, :]
    cf = c + mu_ref[...] * (prev - c)

    r = cf[:, :wd]
    k = cf[:, wd:2 * wd]
    v = cf[:, 2 * wd:3 * wd]
    o = 3 * wd
    xw = cf[:, o:o + r_dec]
    o += r_dec
    xa = cf[:, o:o + r_icl]
    o += r_icl
    xg = cf[:, o:o + r_gate]

    z = w0_ref[...] + _mm(jnp.tanh(xw), w2_ref[...], NN, 2, 2)
    lw = -DECAY_SCALE * _sigmoid(z)
    a = _sigmoid(a0_ref[...] + _mm(xa, a2_ref[...], NN, 2, 2))
    g = _mm(_sigmoid(xg), g2_ref[...], NN, 2, 2)
    if has_vres:
        lo = _mm(_mm(v, v1_ref[...], NN, 2, 2), v2_ref[...], NN, 2, 2)
        v = v + (vf_ref[...] - v) * _sigmoid(v0_ref[...] + lo)

    ones = ones_ref[...]
    kkp = k * kk_ref[...]
    kk = kkp * lax.rsqrt(jnp.maximum(_mm(kkp * kkp, ones, NN, 2, 1), 1e-24))
    k2 = k * (1.0 + (a - 1.0) * ka_ref[...])
    bonus_ref[...] = _mm(r * k2 * rk_ref[...], ones, NN, 2, 1) * v
    g_ref[...] = g
    v_ref[...] = v

    cl = _mm(tril_ref[...], lw, NN, 1, 3)
    lc = _mm(blk_ref[...], lw, NN, 1, 3)
    e_neg = jnp.exp(-cl)
    e_rem = jnp.exp(lc - cl)
    s_at[...] = -kk * jnp.exp(cl - lw)
    s_rt[...] = r * jnp.exp(cl)
    s_bt[...] = kk * a * e_neg
    s_kt[...] = k2 * e_neg
    s_btp[...] = kk * a * e_rem
    s_ktp[...] = k2 * e_rem
    s_pc[...] = jnp.exp(lc)
    s_v[...] = v

    ri = lax.broadcasted_iota(jnp.int32, (CHUNK, CHUNK), 0)
    ci = lax.broadcasted_iota(jnp.int32, (CHUNK, CHUNK), 1)
    strict = ri > ci
    incl = ri >= ci
    eye = (ri == ci).astype(F32)

    def chunk_body(it, carry):
        units = []
        for cc in range(PREP_CHUNKS_PER_STEP):
            rs = pl.ds(pl.multiple_of((it * PREP_CHUNKS_PER_STEP + cc) * CHUNK, CHUNK), CHUNK)
            units += [(rs, slice(h * HEAD, (h + 1) * HEAD)) for h in range(wd // HEAD)]
        nu = len(units)
        sp = SCAN_TERMS
        at = [s_at[u] for u in units]
        rt = [s_rt[u] for u in units]
        vv = [s_v[u] for u in units]
        aa = [_mm(jnp.concatenate([at[i], rt[i]], axis=0),
                  jnp.concatenate([s_bt[u], s_kt[u]], axis=0), NT, sp, sp)
              for i, u in enumerate(units)]
        p = [jnp.where(strict, a[:CHUNK, :CHUNK], 0.0) for a in aa]
        a_ak = [jnp.where(strict, a[:CHUNK, CHUNK:], 0.0) for a in aa]
        a_rb = [jnp.where(incl, a[CHUNK:, :CHUNK], 0.0) for a in aa]
        a_rk = [jnp.where(incl, a[CHUNK:, CHUNK:], 0.0) for a in aa]
        av = [_mm(jnp.concatenate([a_ak[i], a_rk[i]], axis=0), vv[i], NN, sp, sp) for i in range(nu)]
        x = [jnp.concatenate([at[i], av[i][:CHUNK]], axis=1) for i in range(nu)]
        nsteps = CHUNK.bit_length() - 1
        for s in range(nsteps):
            if s + 1 < nsteps:
                px = [_mm(pi, jnp.concatenate([xi, pi], axis=1), NN, sp, sp) for pi, xi in zip(p, x)]
                x = [xi + r[:, :2 * HEAD] for xi, r in zip(x, px)]
                p = [r[:, 2 * HEAD:] for r in px]
            else:
                x = [xi + _mm(pi, xi, NN, sp, sp) for pi, xi in zip(p, x)]
        fin = [_mm(jnp.concatenate([jnp.concatenate([s_btp[u], s_ktp[u]], axis=1).T, a_rb[i]], axis=0),
                   jnp.concatenate([x[i], vv[i]], axis=1), NN, sp, sp)
               for i, u in enumerate(units)]
        for i, u in enumerate(units):
            f = fin[i]
            m_ref[u] = eye * s_pc[u] + f[:HEAD, :HEAD]
            n_ref[u] = f[:HEAD, HEAD:2 * HEAD] + f[HEAD:2 * HEAD, 2 * HEAD:]
            q_ref[u] = rt[i] + f[2 * HEAD:, :HEAD]
            y0_ref[u] = f[2 * HEAD:, HEAD:2 * HEAD] + av[i][CHUNK:]
        return carry

    lax.fori_loop(0, tt // (CHUNK * PREP_CHUNKS_PER_STEP), chunk_body, 0)


def _rwkv_prep(c, v_first, p, vres, bsz, t, tt):
    n, ncols = c.shape
    wd = p["w0"].shape[0]
    r_dec, r_icl, r_gate = p["w2"].shape[0], p["a2"].shape[0], p["g2"].shape[0]
    has_vres = vres is not None
    nt = t // tt
    row = lambda b, j: (b * nt + j, 0)
    const = lambda b, j: (0, 0)
    vec = lambda a: a.reshape(1, -1)
    hid = jnp.arange(wd) // HEAD
    ones = (hid[:, None] == hid[None, :]).astype(BF16)
    ti = jnp.arange(tt)
    same = ti[:, None] // CHUNK == ti[None, :] // CHUNK
    tril = (same & (ti[:, None] >= ti[None, :])).astype(BF16)
    blk = same.astype(BF16)

    args = [c]
    specs = [pl.BlockSpec((tt, ncols), row)]
    if has_vres:
        args.append(v_first)
        specs.append(pl.BlockSpec((tt, wd), row))
    small = [vec(p["mu"]), vec(p["w0"]), p["w2"], vec(p["a0"]), p["a2"], p["g2"],
             vec(p["k_k"]), vec(p["k_a"]), vec(p["r_k"])]
    if has_vres:
        small += [vec(vres[0]), vres[1], vres[2]]
    small += [ones, tril, blk]
    for s in small:
        args.append(s)
        specs.append(pl.BlockSpec(s.shape, const))
    out_spec = pl.BlockSpec((tt, wd), row)
    out_sds = jax.ShapeDtypeStruct((n, wd), F32)
    return pl.pallas_call(
        functools.partial(_rwkv_prep_kernel, tt=tt, wd=wd, r_dec=r_dec, r_icl=r_icl,
                          r_gate=r_gate, has_vres=has_vres),
        grid=(bsz, nt),
        in_specs=specs,
        out_specs=[out_spec] * 7,
        out_shape=[out_sds] * 7,
        scratch_shapes=[pltpu.VMEM((tt + 8, ncols), F32)] + [pltpu.VMEM((tt, wd), F32)] * 8,
        compiler_params=_cparams(("arbitrary", "arbitrary")),
        name="rwkv_prep",
    )(*args)


def _rwkv_scan_kernel(m_ref, n_ref, q_ref, y0_ref, bonus_ref, g_ref, gg_ref, gb_ref, ones_ref,
                      o_ref, z_ref, y_ref, *, bsz, tt, wd):
    j = pl.program_id(0)
    nh = wd // HEAD

    @pl.when(j == 0)
    def _():
        z_ref[...] = jnp.zeros(z_ref.shape, F32)

    for ch in range(tt // CHUNK):
        rs = slice(ch * CHUNK, (ch + 1) * CHUNK)
        for b in range(bsz):
            for h in range(nh):
                hs = slice(h * HEAD, (h + 1) * HEAD)
                z = z_ref[b * nh + h]
                qm = jnp.concatenate([q_ref[b, rs, hs], m_ref[b, rs, hs]], axis=0)
                res = _mm(qm, z, NN, 1, 2)
                y_ref[b, rs, hs] = res[:CHUNK] + y0_ref[b, rs, hs]
                z_ref[b * nh + h] = res[CHUNK:] + n_ref[b, rs, hs]

    ones = ones_ref[...]
    for b in range(bsz):
        y = y_ref[b]
        ym = _mm(y, ones, NN, 2, 1) * (1.0 / HEAD)
        yc = y - ym
        yv = _mm(yc * yc, ones, NN, 2, 1) * (1.0 / HEAD)
        yn = yc * lax.rsqrt(yv + GN_EPS) * gg_ref[...] + gb_ref[...]
        o_ref[b] = (yn + bonus_ref[b]) * g_ref[b]


def _rwkv_scan(m, nn, q, y0, bonus, g, gn_g, gn_b, bsz, t, tt):
    n, wd = m.shape
    nh = wd // HEAD
    hid = jnp.arange(wd) // HEAD
    ones = (hid[:, None] == hid[None, :]).astype(BF16)
    r3 = lambda a: a.reshape(bsz, t, wd)
    blk = pl.BlockSpec((bsz, tt, wd), lambda j: (0, j, 0))
    const = lambda j: (0, 0)
    out = pl.pallas_call(
        functools.partial(_rwkv_scan_kernel, bsz=bsz, tt=tt, wd=wd),
        grid=(t // tt,),
        in_specs=[blk] * 6 + [pl.BlockSpec((1, wd), const), pl.BlockSpec((1, wd), const),
                              pl.BlockSpec((wd, wd), const)],
        out_specs=blk,
        out_shape=jax.ShapeDtypeStruct((bsz, t, wd), F32),
        scratch_shapes=[pltpu.VMEM((bsz * nh, HEAD, HEAD), F32), pltpu.VMEM((bsz, tt, wd), F32)],
        compiler_params=_cparams(("arbitrary",)),
        name="rwkv_scan",
    )(r3(m), r3(nn), r3(q), r3(y0), r3(bonus), r3(g), gn_g.reshape(1, wd), gn_b.reshape(1, wd), ones)
    return out.reshape(n, wd)


def _outproj_kernel(*refs, n_pc, route, n_exp):
    if route:
        (ypc_ref, yr_ref, h_ref, w_ref, g_ref, rt_ref, tri_ref,
         ho_ref, hn_ref, route_ref, cnt_ref, carry) = refs
    else:
        ypc_ref, yr_ref, h_ref, w_ref, g_ref, ho_ref, hn_ref = refs
    mix = (jnp.dot(ypc_ref[...].astype(BF16), w_ref[:n_pc, :], preferred_element_type=F32)
           + jnp.dot(yr_ref[...].astype(BF16), w_ref[n_pc:, :], preferred_element_type=F32))
    h = h_ref[...] + mix
    ho_ref[...] = h
    ms = jnp.mean(h * h, axis=-1, keepdims=True)
    hn = h * lax.rsqrt(ms + RMS_EPS) * g_ref[...]
    hn_ref[...] = hn.astype(hn_ref.dtype)
    if not route:
        return

    i = pl.program_id(0)

    @pl.when(i == 0)
    def _():
        carry[...] = jnp.zeros(carry.shape, F32)

    tm = hn.shape[0]
    logits = _mm(hn, rt_ref[...], NN, 2, 2)
    lane = lax.broadcasted_iota(jnp.int32, (tm, LANES), 1).astype(F32)
    neg = jnp.float32(-jnp.inf)
    logits = jnp.where(lane < n_exp, logits, neg)
    m1 = jnp.max(logits, axis=-1, keepdims=True)
    i1 = jnp.min(jnp.where(logits == m1, lane, float(LANES)), axis=-1, keepdims=True)
    rest = jnp.where(lane == i1, neg, logits)
    m2 = jnp.max(rest, axis=-1, keepdims=True)
    i2 = jnp.min(jnp.where(rest == m2, lane, float(LANES)), axis=-1, keepdims=True)
    e = jnp.exp(m2 - m1)
    g1 = 1.0 / (1.0 + e)
    g2 = e / (1.0 + e)
    oh1 = (lane == i1).astype(F32)
    oh2 = (lane == i2).astype(F32)
    both = oh1 + oh2
    before = carry[0:1, :] + _mm(tri_ref[...], both.astype(BF16), NN, 1, 1)
    rank1 = jnp.sum(before * oh1, axis=-1, keepdims=True)
    rank2 = jnp.sum(before * oh2, axis=-1, keepdims=True)
    tot = carry[0:1, :] + jnp.sum(both, axis=0, keepdims=True)
    carry[...] = jnp.broadcast_to(tot, carry.shape)
    cnt_ref[...] = jnp.broadcast_to(tot, cnt_ref.shape)
    out = jnp.where(lane == 0, g1, 0.0)
    out = jnp.where(lane == 1, g2, out)
    out = jnp.where(lane == 2, i1, out)
    out = jnp.where(lane == 3, i2, out)
    out = jnp.where(lane == 4, rank1, out)
    out = jnp.where(lane == 5, rank2, out)
    route_ref[...] = out


def _outproj(ypc, yr, h, w_bf16, g, router, tm, hn_dtype):
    n, d = h.shape
    n_pc, n_r = ypc.shape[1], yr.shape[1]
    route = router is not None
    rowb = lambda w: pl.BlockSpec((tm, w), lambda i: (i, 0))
    const = lambda i: (0, 0)
    args = [ypc, yr, h, w_bf16, g.reshape(1, d)]
    specs = [rowb(n_pc), rowb(n_r), rowb(d), pl.BlockSpec((d, d), const), pl.BlockSpec((1, d), const)]
    out_specs = [rowb(d), rowb(d)]
    out_shape = [jax.ShapeDtypeStruct((n, d), F32), jax.ShapeDtypeStruct((n, d), hn_dtype)]
    scratch = []
    n_exp = 0
    if route:
        n_exp = router.shape[1]
        rt = jnp.zeros((d, LANES), F32).at[:, :n_exp].set(router)
        ti = jnp.arange(tm)
        tri = (ti[:, None] > ti[None, :]).astype(BF16)
        args += [rt, tri]
        specs += [pl.BlockSpec((d, LANES), const), pl.BlockSpec((tm, tm), const)]
        out_specs += [rowb(LANES), pl.BlockSpec((8, LANES), const)]
        out_shape += [jax.ShapeDtypeStruct((n, LANES), F32), jax.ShapeDtypeStruct((8, LANES), F32)]
        scratch = [pltpu.VMEM((8, LANES), F32)]
    return pl.pallas_call(
        functools.partial(_outproj_kernel, n_pc=n_pc, route=route, n_exp=n_exp),
        grid=(n // tm,),
        in_specs=specs,
        out_specs=out_specs,
        out_shape=out_shape,
        scratch_shapes=scratch,
        compiler_params=_cparams(("arbitrary",)),
        name="outproj_route" if route else "outproj",
    )(*args)


def _ffn_kernel(be_ref, nu_ref, *refs, has_res):
    if has_res:
        x_ref, wg_ref, wu_ref, wd_ref, res_ref, o_ref = refs
    else:
        x_ref, wg_ref, wu_ref, wd_ref, o_ref = refs
    del be_ref
    used = pl.program_id(0) < nu_ref[0]

    @pl.when(used)
    def _():
        x = x_ref[...].astype(BF16)
        gate = jnp.dot(x, wg_ref[0], preferred_element_type=F32)
        up = jnp.dot(x, wu_ref[0], preferred_element_type=F32)
        mid = (_silu(gate) * up).astype(BF16)
        y = jnp.dot(mid, wd_ref[0], preferred_element_type=F32)
        o_ref[...] = y + res_ref[...] if has_res else y

    @pl.when(jnp.logical_not(used))
    def _():
        o_ref[...] = res_ref[...] if has_res else jnp.zeros(o_ref.shape, F32)


def _ffn(x, wg, wu, wd, block_e, n_used, res, tr):
    rows, d = x.shape
    _, _, fdim = wg.shape
    has_res = res is not None
    resident = pl.Buffered(1)
    args = [x, wg, wu, wd]
    specs = [
        pl.BlockSpec((tr, d), lambda i, be, nu: (i, 0)),
        pl.BlockSpec((1, d, fdim), lambda i, be, nu: (be[i], 0, 0), pipeline_mode=resident),
        pl.BlockSpec((1, d, fdim), lambda i, be, nu: (be[i], 0, 0), pipeline_mode=resident),
        pl.BlockSpec((1, fdim, d), lambda i, be, nu: (be[i], 0, 0), pipeline_mode=resident),
    ]
    if has_res:
        args.append(res)
        specs.append(pl.BlockSpec((tr, d), lambda i, be, nu: (i, 0)))
    return pl.pallas_call(
        functools.partial(_ffn_kernel, has_res=has_res),
        grid_spec=pltpu.PrefetchScalarGridSpec(
            num_scalar_prefetch=2,
            grid=(rows // tr,),
            in_specs=specs,
            out_specs=pl.BlockSpec((tr, d), lambda i, be, nu: (i, 0)),
        ),
        out_shape=jax.ShapeDtypeStruct((rows, d), F32),
        compiler_params=_cparams(("arbitrary",)),
        name="ffn_res" if has_res else "ffn_grouped",
    )(block_e, n_used, *args)


def _row_copy(src_ref, s, dst_ref, d, sem):
    return pltpu.make_async_copy(src_ref.at[pl.ds(s, 1), :], dst_ref.at[pl.ds(d, 1), :], sem)


def _dispatch_kernel(dest_ref, x_ref, xd_in_ref, xd_ref, sem, *, tm):
    del xd_in_ref

    def issue(r, carry):
        for kk in range(N_TOP):
            _row_copy(x_ref, r, xd_ref, dest_ref[0, 0, N_TOP * r + kk], sem).start()
        return carry

    lax.fori_loop(0, tm, issue, 0, unroll=DMA_UNROLL)

    def drain(r, carry):
        for kk in range(N_TOP):
            _row_copy(x_ref, 0, xd_ref, 0, sem).wait()
        return carry

    lax.fori_loop(0, tm, drain, 0, unroll=DMA_UNROLL)


def _dispatch(x, dest, n_rows, tm):
    n, d = x.shape
    nblk = n // tm
    dest3 = dest.reshape(nblk, 1, N_TOP * tm)
    xd0 = jnp.zeros((n_rows, d), x.dtype)
    return pl.pallas_call(
        functools.partial(_dispatch_kernel, tm=tm),
        grid=(nblk,),
        in_specs=[
            pl.BlockSpec((1, 1, N_TOP * tm), lambda i: (i, 0, 0), memory_space=pltpu.SMEM),
            pl.BlockSpec((tm, d), lambda i: (i, 0)),
            pl.BlockSpec(memory_space=pl.ANY),
        ],
        out_specs=pl.BlockSpec(memory_space=pl.ANY),
        out_shape=jax.ShapeDtypeStruct((n_rows, d), x.dtype),
        scratch_shapes=[pltpu.SemaphoreType.DMA(())],
        input_output_aliases={2: 0},
        compiler_params=_cparams(("arbitrary",)),
        name="moe_dispatch",
    )(dest3, x, xd0)


def _combine_kernel(dest_ref, yd_ref, gate_ref, h_ref, fg_ref, o_ref, buf, sem, *, tm, final_norm):
    def issue(r, carry):
        for kk in range(N_TOP):
            _row_copy(yd_ref, dest_ref[0, 0, N_TOP * r + kk], buf.at[kk], r, sem).start()
        return carry

    lax.fori_loop(0, tm, issue, 0, unroll=DMA_UNROLL)

    def drain(r, carry):
        for kk in range(N_TOP):
            _row_copy(yd_ref, 0, buf.at[kk], 0, sem).wait()
        return carry

    lax.fori_loop(0, tm, drain, 0, unroll=DMA_UNROLL)

    gate = gate_ref[...]
    h = h_ref[...]
    for kk in range(N_TOP):
        h = h + buf[kk] * gate[:, kk:kk + 1]
    if final_norm:
        ms = jnp.mean(h * h, axis=-1, keepdims=True)
        h = h * lax.rsqrt(ms + RMS_EPS) * fg_ref[...]
    o_ref[...] = h


def _combine(yd, dest, gates, h, final_g, tm, final_norm):
    n, d = h.shape
    nblk = n // tm
    dest3 = dest.reshape(nblk, 1, N_TOP * tm)
    return pl.pallas_call(
        functools.partial(_combine_kernel, tm=tm, final_norm=final_norm),
        grid=(nblk,),
        in_specs=[
            pl.BlockSpec((1, 1, N_TOP * tm), lambda i: (i, 0, 0), memory_space=pltpu.SMEM),
            pl.BlockSpec(memory_space=pl.ANY),
            pl.BlockSpec((tm, LANES), lambda i: (i, 0)),
            pl.BlockSpec((tm, d), lambda i: (i, 0)),
            pl.BlockSpec((1, d), lambda i: (0, 0)),
        ],
        out_specs=pl.BlockSpec((tm, d), lambda i: (i, 0)),
        out_shape=jax.ShapeDtypeStruct((n, d), F32),
        scratch_shapes=[pltpu.VMEM((N_TOP, tm, d), F32), pltpu.SemaphoreType.DMA(())],
        compiler_params=_cparams(("arbitrary",)),
        name="moe_combine",
    )(dest3, yd, gates, h, final_g.reshape(1, d))


def _rmsnorm_kernel(x_ref, g_ref, o_ref):
    x = x_ref[...]
    ms = jnp.mean(x * x, axis=-1, keepdims=True)
    o_ref[...] = x * lax.rsqrt(ms + RMS_EPS) * g_ref[...]


def _rmsnorm(x, g, tm):
    n, d = x.shape
    return pl.pallas_call(
        _rmsnorm_kernel,
        grid=(n // tm,),
        in_specs=[pl.BlockSpec((tm, d), lambda i: (i, 0)), pl.BlockSpec((1, d), lambda i: (0, 0))],
        out_specs=pl.BlockSpec((tm, d), lambda i: (i, 0)),
        out_shape=jax.ShapeDtypeStruct((n, d), F32),
        compiler_params=_cparams(("arbitrary",)),
        name="final_rmsnorm",
    )(x, g.reshape(1, d))


def _pick(total, target, mult):
    best = None
    for cand in range(mult, min(total, target) + 1, mult):
        if total % cand == 0:
            best = cand
    assert best is not None, (total, target, mult)
    return best


def kernel(x, norm1_g, w_in, pool_w, pool_scale, conv_w, conv_b, conv_ln_g, conv_ln_b, shift_mu, rwkv_w0, rwkv_w2, rwkv_a0, rwkv_a2, rwkv_g2, rwkv_k_k, rwkv_k_a, rwkv_r_k, rwkv_gn_g, rwkv_gn_b, rwkv_v0, rwkv_v1, rwkv_v2, w_out, norm2_g, ffn_w_gate, ffn_w_up, ffn_w_down, moe_router, moe_w_gate, moe_w_up, moe_w_down, final_g):
    bsz, t, d = x.shape
    depth = w_in.shape[0]
    n = bsz * t
    n_pool = pool_scale.shape[1]
    n_conv = 2 * conv_b.shape[1]
    assert t % CHUNK == 0

    tm = _pick(n, 512, 8)
    tt_pc = _pick(t, 512, HALO)
    tt_prep = _pick(t, 256, CHUNK)
    tt_scan = _pick(t, 128, CHUNK)
    tm_moe = _pick(n, 256, 8)

    h = x.reshape(n, d)
    v_first = None
    for l in range(depth):
        pp, pc, prw = _inproj(h, norm1_g[l], w_in[l].astype(BF16), n_pool, n_conv, tm)
        ypc = _poolconv(pp, pc, pool_w[l], pool_scale[l], conv_w[l], conv_b[l],
                        conv_ln_g[l], conv_ln_b[l], bsz, t, tt_pc)
        rp = dict(mu=shift_mu[l], w0=rwkv_w0[l], w2=rwkv_w2[l], a0=rwkv_a0[l], a2=rwkv_a2[l],
                  g2=rwkv_g2[l], k_k=rwkv_k_k[l], k_a=rwkv_k_a[l], r_k=rwkv_r_k[l])
        vres = None if l == 0 else (rwkv_v0[l - 1], rwkv_v1[l - 1], rwkv_v2[l - 1])
        m, nn, q, y0, bonus, g, v_l = _rwkv_prep(prw, v_first, rp, vres, bsz, t, tt_prep)
        if l == 0:
            v_first = v_l
        yr = _rwkv_scan(m, nn, q, y0, bonus, g, rwkv_gn_g[l], rwkv_gn_b[l], bsz, t, tt_scan)

        i = l // 2
        last = l == depth - 1
        if l % 2 == 0:
            h_mid, hn = _outproj(ypc, yr, h, w_out[l].astype(BF16), norm2_g[l], None, tm, BF16)
            tr = _pick(n, 512, 8)
            nblk = n // tr
            h = _ffn(hn, ffn_w_gate[i][None].astype(BF16), ffn_w_up[i][None].astype(BF16),
                     ffn_w_down[i][None].astype(BF16), jnp.zeros((nblk,), jnp.int32),
                     jnp.full((1,), nblk, jnp.int32), h_mid, tr)
            if last:
                h = _rmsnorm(h, final_g, tm)
        else:
            n_exp = moe_router.shape[2]
            h_mid, hn, route, cnt = _outproj(ypc, yr, h, w_out[l].astype(BF16), norm2_g[l],
                                             moe_router[i], tm, F32)
            gates = route
            top_e = route[:, 2:2 + N_TOP].astype(jnp.int32)
            rank = route[:, 4:4 + N_TOP].astype(jnp.int32)
            counts = cnt[0, :n_exp].astype(jnp.int32)
            tr = _pick(n, 512, 8)
            padded = (counts + tr - 1) // tr * tr
            pad_ends = jnp.cumsum(padded)
            pad_starts = pad_ends - padded
            dest = pad_starts[top_e] + rank
            nblk = (n * N_TOP) // tr + n_exp
            block_e = jnp.minimum(
                jnp.searchsorted(pad_ends, jnp.arange(nblk, dtype=jnp.int32) * tr, side="right"),
                n_exp - 1).astype(jnp.int32)
            n_used = (pad_ends[-1] // tr).astype(jnp.int32).reshape(1)
            xd = _dispatch(hn, dest, nblk * tr, tm_moe)
            yd = _ffn(xd, moe_w_gate[i].astype(BF16), moe_w_up[i].astype(BF16),
                      moe_w_down[i].astype(BF16), block_e, n_used, None, tr)
            h = _combine(yd, dest, gates, h_mid, final_g, tm_moe, last)
    return h.reshape(bsz, t, d)
```

```python
import functools

import jax
import jax.numpy as jnp
from jax import lax
from jax.experimental import pallas as pl
from jax.experimental.pallas import tpu as pltpu

F32 = jnp.float32
BF16 = jnp.bfloat16

HEAD = 64
CHUNK = 64
POOL_WINDOWS = (2, 4, 8, 16)
CONV_KERNEL = 31
HALO = 32
RMS_EPS = 1e-6
LN_EPS = 1e-5
GN_EPS = 64e-5
DECAY_SCALE = 0.6065306597126334
N_TOP = 2
LANES = 128
SUBLANES = 8
VMEM_LIMIT = 56 * 1024 * 1024
SCAN_TERMS = 1
PREP_CHUNKS_PER_STEP = 4

NN = ((1,), (0,))
NT = ((1,), (1,))
TN = ((0,), (0,))


def _split(x, n):
    if x.dtype == BF16:
        return [x]
    parts = []
    rem = x
    for i in range(n):
        p = rem.astype(BF16)
        parts.append(p)
        if i + 1 < n:
            rem = rem - p.astype(F32)
    return parts


def _mm(a, b, dims=NN, na=1, nb=1):
    pa = _split(a, na)
    pb = _split(b, nb)
    nmax = max(len(pa), len(pb))
    out = None
    for i, x in enumerate(pa):
        for j, y in enumerate(pb):
            if i + j < nmax:
                t = lax.dot_general(x, y, (dims, ((), ())), preferred_element_type=F32)
                out = t if out is None else out + t
    return out


def _sigmoid(x):
    return 0.5 * jnp.tanh(0.5 * x) + 0.5


def _silu(x):
    return x * _sigmoid(x)


def _cparams(sem):
    return pltpu.CompilerParams(dimension_semantics=sem, vmem_limit_bytes=VMEM_LIMIT)


def _inproj_kernel(x_ref, g_ref, w_ref, op_ref, oc_ref, or_ref, *, n_pool, n_conv):
    x = x_ref[...]
    ms = jnp.mean(x * x, axis=-1, keepdims=True)
    y = x * lax.rsqrt(ms + RMS_EPS) * g_ref[...]
    p = jnp.dot(y.astype(BF16), w_ref[...], preferred_element_type=F32)
    op_ref[...] = p[:, :n_pool]
    oc_ref[...] = p[:, n_pool:n_pool + n_conv]
    or_ref[...] = p[:, n_pool + n_conv:]


def _inproj(h, g, w_bf16, n_pool, n_conv, tm):
    n, d = h.shape
    n_in = w_bf16.shape[1]
    n_rw = n_in - n_pool - n_conv
    return pl.pallas_call(
        functools.partial(_inproj_kernel, n_pool=n_pool, n_conv=n_conv),
        grid=(n // tm,),
        in_specs=[
            pl.BlockSpec((tm, d), lambda i: (i, 0)),
            pl.BlockSpec((1, d), lambda i: (0, 0)),
            pl.BlockSpec((d, n_in), lambda i: (0, 0)),
        ],
        out_specs=[
            pl.BlockSpec((tm, n_pool), lambda i: (i, 0)),
            pl.BlockSpec((tm, n_conv), lambda i: (i, 0)),
            pl.BlockSpec((tm, n_rw), lambda i: (i, 0)),
        ],
        out_shape=[
            jax.ShapeDtypeStruct((n, n_pool), F32),
            jax.ShapeDtypeStruct((n, n_conv), F32),
            jax.ShapeDtypeStruct((n, n_rw), F32),
        ],
        compiler_params=_cparams(("arbitrary",)),
        name="inproj",
    )(h, g.reshape(1, d), w_bf16)


def _poolconv_kernel(pp_ref, pc_ref, pw_ref, ps_ref, wl_ref, cw_ref, cb_ref, lg_ref, lb_ref,
                     o_ref, extp, extc, psum, rot, *, tt, cp, cc):
    j = pl.program_id(1)
    lext = HALO + tt

    @pl.when(j == 0)
    def _():
        extp[0:HALO, :] = jnp.zeros((HALO, cp), F32)
        extc[0:HALO, :] = jnp.zeros((HALO, cc), F32)

    u = pp_ref[...]
    extp[HALO:HALO + tt, :] = u
    wl = wl_ref[...]
    sel = jnp.zeros_like(u)
    cur, w, lo = extp, 1, 0
    for idx, win in enumerate(POOL_WINDOWS):
        assert win == 2 * w
        lo = -(-(lo + w) // SUBLANES) * SUBLANES
        assert lo <= HALO
        val = cur[lo:lext, :] + cur[pl.ds(lo - w, lext - lo), :]
        if idx + 1 < len(POOL_WINDOWS):
            psum[idx, lo:lext, :] = val
            cur = psum.at[idx]
        sel = jnp.where(wl == float(win), val[HALO - lo:], sel)
        w = win
    pos = (j * tt + lax.broadcasted_iota(jnp.int32, (tt, 1), 0) + 1).astype(F32)
    inv = jnp.where(pos >= wl, 1.0 / wl, 1.0 / pos)
    d = sel * inv - u
    yp = _mm(d, pw_ref[...], NN, 2, 2) * ps_ref[...]
    extp[0:HALO, :] = extp[tt:tt + HALO, :]

    pc = pc_ref[...]
    uc = pc[:, :cc] * _sigmoid(pc[:, cc:])
    extc[HALO:HALO + tt, :] = uc
    nrot = lext - SUBLANES
    for r in range(1, SUBLANES):
        rot[r - 1] = extc[pl.ds(r, nrot), :]
    acc = jnp.zeros((tt, cc), F32)
    for t in range(CONV_KERNEL):
        q, r = divmod(HALO - (CONV_KERNEL - 1) + t, SUBLANES)
        if r == 0:
            tap = extc[pl.ds(SUBLANES * q, tt), :]
        else:
            tap = rot[r - 1, pl.ds(SUBLANES * q, tt), :]
        acc = acc + cw_ref[t:t + 1, :] * tap
    extc[0:HALO, :] = extc[tt:tt + HALO, :]
    yf = acc + cb_ref[...]
    mu = jnp.mean(yf, axis=-1, keepdims=True)
    yc = yf - mu
    var = jnp.mean(yc * yc, axis=-1, keepdims=True)
    yn = yc * lax.rsqrt(var + LN_EPS) * lg_ref[...] + lb_ref[...]
    o_ref[:, :cp] = yp
    o_ref[:, cp:] = _silu(yn)


def _poolconv(pp, pc, pool_w, pool_scale, conv_w, conv_b, ln_g, ln_b, bsz, t, tt):
    n, cp = pp.shape
    cc = pc.shape[1] // 2
    ng, cg, _ = pool_w.shape
    pw = jnp.zeros((cp, cp), F32)
    for gi in range(ng):
        pw = pw.at[gi * cg:(gi + 1) * cg, gi * cg:(gi + 1) * cg].set(pool_w[gi])
    wl = jnp.repeat(jnp.asarray(POOL_WINDOWS, F32), cg).reshape(1, cp)
    nt = t // tt
    row = lambda b, j: (b * nt + j, 0)
    const = lambda b, j: (0, 0)
    return pl.pallas_call(
        functools.partial(_poolconv_kernel, tt=tt, cp=cp, cc=cc),
        grid=(bsz, nt),
        in_specs=[
            pl.BlockSpec((tt, cp), row),
            pl.BlockSpec((tt, 2 * cc), row),
            pl.BlockSpec((cp, cp), const),
            pl.BlockSpec((1, cp), const),
            pl.BlockSpec((1, cp), const),
            pl.BlockSpec((CONV_KERNEL, cc), const),
            pl.BlockSpec((1, cc), const),
            pl.BlockSpec((1, cc), const),
            pl.BlockSpec((1, cc), const),
        ],
        out_specs=pl.BlockSpec((tt, cp + cc), row),
        out_shape=jax.ShapeDtypeStruct((n, cp + cc), F32),
        scratch_shapes=[pltpu.VMEM((tt + HALO, cp), F32), pltpu.VMEM((tt + HALO, cc), F32),
                        pltpu.VMEM((len(POOL_WINDOWS) - 1, tt + HALO, cp), F32),
                        pltpu.VMEM((SUBLANES - 1, tt + HALO - SUBLANES, cc), F32)],
        compiler_params=_cparams(("arbitrary", "arbitrary")),
        name="poolconv",
    )(pp, pc, pw, pool_scale.reshape(1, cp), wl, conv_w, conv_b.reshape(1, cc),
      ln_g.reshape(1, cc), ln_b.reshape(1, cc))


def _rwkv_prep_kernel(*refs, tt, wd, r_dec, r_icl, r_gate, has_vres):
    it = iter(refs)
    c_ref = next(it)
    vf_ref = next(it) if has_vres else None
    mu_ref, w0_ref, w2_ref, a0_ref, a2_ref, g2_ref, kk_ref, ka_ref, rk_ref = (next(it) for _ in range(9))
    if has_vres:
        v0_ref, v1_ref, v2_ref = (next(it) for _ in range(3))
    ones_ref, tril_ref, blk_ref = next(it), next(it), next(it)
    m_ref, n_ref, q_ref, y0_ref, bonus_ref, g_ref, v_ref = (next(it) for _ in range(7))
    ext = next(it)
    s_at, s_rt, s_bt, s_kt, s_btp, s_ktp, s_pc, s_v = (next(it) for _ in range(8))

    j = pl.program_id(1)
    ncols = c_ref.shape[1]

    @pl.when(j == 0)
    def _():
        ext[0:8, :] = jnp.zeros((8, ncols), F32)

    c = c_ref[...]
    ext[8:8 + tt, :] = c
    prev = ext[pl.ds(7, tt), :]
    ext[0:8, :] = ext[tt:tt + 8, :]
    cf = c + mu_ref[...] * (prev - c)

    r = cf[:, :wd]
    k = cf[:, wd:2 * wd]
    v = cf[:, 2 * wd:3 * wd]
    o = 3 * wd
    xw = cf[:, o:o + r_dec]
    o += r_dec
    xa = cf[:, o:o + r_icl]
    o += r_icl
    xg = cf[:, o:o + r_gate]

    z = w0_ref[...] + _mm(jnp.tanh(xw), w2_ref[...], NN, 2, 2)
    lw = -DECAY_SCALE * _sigmoid(z)
    a = _sigmoid(a0_ref[...] + _mm(xa, a2_ref[...], NN, 2, 2))
    g = _mm(_sigmoid(xg), g2_ref[...], NN, 2, 2)
    if has_vres:
        lo = _mm(_mm(v, v1_ref[...], NN, 2, 2), v2_ref[...], NN, 2, 2)
        v = v + (vf_ref[...] - v) * _sigmoid(v0_ref[...] + lo)

    ones = ones_ref[...]
    kkp = k * kk_ref[...]
    kk = kkp * lax.rsqrt(jnp.maximum(_mm(kkp * kkp, ones, NN, 2, 1), 1e-24))
    k2 = k * (1.0 + (a - 1.0) * ka_ref[...])
    bonus_ref[...] = _mm(r * k2 * rk_ref[...], ones, NN, 2, 1) * v
    g_ref[...] = g
    v_ref[...] = v

    cl = _mm(tril_ref[...], lw, NN, 1, 3)
    lc = _mm(blk_ref[...], lw, NN, 1, 3)
    e_neg = jnp.exp(-cl)
    e_rem = jnp.exp(lc - cl)
    s_at[...] = -kk * jnp.exp(cl - lw)
    s_rt[...] = r * jnp.exp(cl)
    s_bt[...] = kk * a * e_neg
    s_kt[...] = k2 * e_neg
    s_btp[...] = kk * a * e_rem
    s_ktp[...] = k2 * e_rem
    s_pc[...] = jnp.exp(lc)
    s_v[...] = v

    ri = lax.broadcasted_iota(jnp.int32, (CHUNK, CHUNK), 0)
    ci = lax.broadcasted_iota(jnp.int32, (CHUNK, CHUNK), 1)
    strict = ri > ci
    incl = ri >= ci
    eye = (ri == ci).astype(F32)

    def chunk_body(it, carry):
        units = []
        for cc in range(PREP_CHUNKS_PER_STEP):
            rs = pl.ds(pl.multiple_of((it * PREP_CHUNKS_PER_STEP + cc) * CHUNK, CHUNK), CHUNK)
            units += [(rs, slice(h * HEAD, (h + 1) * HEAD)) for h in range(wd // HEAD)]
        nu = len(units)
        sp = SCAN_TERMS
        at = [s_at[u] for u in units]
        rt = [s_rt[u] for u in units]
        vv = [s_v[u] for u in units]
        aa = [_mm(jnp.concatenate([at[i], rt[i]], axis=0),
                  jnp.concatenate([s_bt[u], s_kt[u]], axis=0), NT, sp, sp)
              for i, u in enumerate(units)]
        p = [jnp.where(strict, a[:CHUNK, :CHUNK], 0.0) for a in aa]
        a_ak = [jnp.where(strict, a[:CHUNK, CHUNK:], 0.0) for a in aa]
        a_rb = [jnp.where(incl, a[CHUNK:, :CHUNK], 0.0) for a in aa]
        a_rk = [jnp.where(incl, a[CHUNK:, CHUNK:], 0.0) for a in aa]
        av = [_mm(jnp.concatenate([a_ak[i], a_rk[i]], axis=0), vv[i], NN, sp, sp) for i in range(nu)]
        yv = [a[CHUNK:] for a in av]
        x = [jnp.concatenate([at[i], av[i][:CHUNK]], axis=1) for i in range(nu)]
        nsteps = CHUNK.bit_length() - 1
        for s in range(nsteps):
            if s + 1 < nsteps:
                px = [_mm(pi, jnp.concatenate([xi, pi], axis=1), NN, sp, sp) for pi, xi in zip(p, x)]
                x = [xi + r[:, :2 * HEAD] for xi, r in zip(x, px)]
                p = [r[:, 2 * HEAD:] for r in px]
            else:
                x = [xi + _mm(pi, xi, NN, sp, sp) for pi, xi in zip(p, x)]
        fin = [_mm(jnp.concatenate([jnp.concatenate([s_btp[u], s_ktp[u]], axis=1).T, a_rb[i]], axis=0),
                   jnp.concatenate([x[i], vv[i]], axis=1), NN, sp, sp)
               for i, u in enumerate(units)]
        for i, u in enumerate(units):
            f = fin[i]
            m_ref[u] = eye * s_pc[u] + f[:HEAD, :HEAD]
            n_ref[u] = f[:HEAD, HEAD:2 * HEAD] + f[HEAD:2 * HEAD, 2 * HEAD:]
            q_ref[u] = rt[i] + f[2 * HEAD:, :HEAD]
            y0_ref[u] = f[2 * HEAD:, HEAD:2 * HEAD] + yv[i]
        return carry

    lax.fori_loop(0, tt // (CHUNK * PREP_CHUNKS_PER_STEP), chunk_body, 0)


def _rwkv_prep(c, v_first, p, vres, bsz, t, tt):
    n, ncols = c.shape
    wd = p["w0"].shape[0]
    r_dec, r_icl, r_gate = p["w2"].shape[0], p["a2"].shape[0], p["g2"].shape[0]
    has_vres = vres is not None
    nt = t // tt
    row = lambda b, j: (b * nt + j, 0)
    const = lambda b, j: (0, 0)
    vec = lambda a: a.reshape(1, -1)
    hid = jnp.arange(wd) // HEAD
    ones = (hid[:, None] == hid[None, :]).astype(BF16)
    ti = jnp.arange(tt)
    same = ti[:, None] // CHUNK == ti[None, :] // CHUNK
    tril = (same & (ti[:, None] >= ti[None, :])).astype(BF16)
    blk = same.astype(BF16)

    args = [c]
    specs = [pl.BlockSpec((tt, ncols), row)]
    if has_vres:
        args.append(v_first)
        specs.append(pl.BlockSpec((tt, wd), row))
    small = [vec(p["mu"]), vec(p["w0"]), p["w2"], vec(p["a0"]), p["a2"], p["g2"],
             vec(p["k_k"]), vec(p["k_a"]), vec(p["r_k"])]
    if has_vres:
        small += [vec(vres[0]), vres[1], vres[2]]
    small += [ones, tril, blk]
    for s in small:
        args.append(s)
        specs.append(pl.BlockSpec(s.shape, const))
    out_spec = pl.BlockSpec((tt, wd), row)
    out_sds = jax.ShapeDtypeStruct((n, wd), F32)
    return pl.pallas_call(
        functools.partial(_rwkv_prep_kernel, tt=tt, wd=wd, r_dec=r_dec, r_icl=r_icl,
                          r_gate=r_gate, has_vres=has_vres),
        grid=(bsz, nt),
        in_specs=specs,
        out_specs=[out_spec] * 7,
        out_shape=[out_sds] * 7,
        scratch_shapes=[pltpu.VMEM((tt + 8, ncols), F32)] + [pltpu.VMEM((tt, wd), F32)] * 8,
        compiler_params=_cparams(("arbitrary", "arbitrary")),
        name="rwkv_prep",
    )(*args)


def _rwkv_scan_kernel(m_ref, n_ref, q_ref, y0_ref, bonus_ref, g_ref, gg_ref, gb_ref, ones_ref,
                      o_ref, z_ref, y_ref, *, bsz, tt, wd):
    j = pl.program_id(0)
    nh = wd // HEAD

    @pl.when(j == 0)
    def _():
        z_ref[...] = jnp.zeros(z_ref.shape, F32)

    for ch in range(tt // CHUNK):
        rs = slice(ch * CHUNK, (ch + 1) * CHUNK)
        for b in range(bsz):
            for h in range(nh):
                hs = slice(h * HEAD, (h + 1) * HEAD)
                z = z_ref[b * nh + h]
                qm = jnp.concatenate([q_ref[b, rs, hs], m_ref[b, rs, hs]], axis=0)
                res = _mm(qm, z, NN, 1, 2)
                y_ref[b, rs, hs] = res[:CHUNK] + y0_ref[b, rs, hs]
                z_ref[b * nh + h] = res[CHUNK:] + n_ref[b, rs, hs]

    ones = ones_ref[...]
    for b in range(bsz):
        y = y_ref[b]
        ym = _mm(y, ones, NN, 2, 1) * (1.0 / HEAD)
        yc = y - ym
        yv = _mm(yc * yc, ones, NN, 2, 1) * (1.0 / HEAD)
        yn = yc * lax.rsqrt(yv + GN_EPS) * gg_ref[...] + gb_ref[...]
        o_ref[b] = (yn + bonus_ref[b]) * g_ref[b]


def _rwkv_scan(m, nn, q, y0, bonus, g, gn_g, gn_b, bsz, t, tt):
    n, wd = m.shape
    nh = wd // HEAD
    hid = jnp.arange(wd) // HEAD
    ones = (hid[:, None] == hid[None, :]).astype(BF16)
    r3 = lambda a: a.reshape(bsz, t, wd)
    blk = pl.BlockSpec((bsz, tt, wd), lambda j: (0, j, 0))
    const = lambda j: (0, 0)
    out = pl.pallas_call(
        functools.partial(_rwkv_scan_kernel, bsz=bsz, tt=tt, wd=wd),
        grid=(t // tt,),
        in_specs=[blk] * 6 + [pl.BlockSpec((1, wd), const), pl.BlockSpec((1, wd), const),
                              pl.BlockSpec((wd, wd), const)],
        out_specs=blk,
        out_shape=jax.ShapeDtypeStruct((bsz, t, wd), F32),
        scratch_shapes=[pltpu.VMEM((bsz * nh, HEAD, HEAD), F32), pltpu.VMEM((bsz, tt, wd), F32)],
        compiler_params=_cparams(("arbitrary",)),
        name="rwkv_scan",
    )(r3(m), r3(nn), r3(q), r3(y0), r3(bonus), r3(g), gn_g.reshape(1, wd), gn_b.reshape(1, wd), ones)
    return out.reshape(n, wd)


def _outproj_kernel(*refs, n_pc, route, n_exp):
    if route:
        (ypc_ref, yr_ref, h_ref, w_ref, g_ref, rt_ref, tri_ref,
         ho_ref, hn_ref, route_ref, cnt_ref, carry) = refs
    else:
        ypc_ref, yr_ref, h_ref, w_ref, g_ref, ho_ref, hn_ref = refs
    mix = (jnp.dot(ypc_ref[...].astype(BF16), w_ref[:n_pc, :], preferred_element_type=F32)
           + jnp.dot(yr_ref[...].astype(BF16), w_ref[n_pc:, :], preferred_element_type=F32))
    h = h_ref[...] + mix
    ho_ref[...] = h
    ms = jnp.mean(h * h, axis=-1, keepdims=True)
    hn = h * lax.rsqrt(ms + RMS_EPS) * g_ref[...]
    hn_ref[...] = hn.astype(hn_ref.dtype)
    if not route:
        return

    i = pl.program_id(0)

    @pl.when(i == 0)
    def _():
        carry[...] = jnp.zeros(carry.shape, F32)

    tm = hn.shape[0]
    logits = _mm(hn, rt_ref[...], NN, 2, 2)
    lane = lax.broadcasted_iota(jnp.int32, (tm, LANES), 1).astype(F32)
    neg = jnp.float32(-jnp.inf)
    logits = jnp.where(lane < n_exp, logits, neg)
    m1 = jnp.max(logits, axis=-1, keepdims=True)
    i1 = jnp.min(jnp.where(logits == m1, lane, float(LANES)), axis=-1, keepdims=True)
    rest = jnp.where(lane == i1, neg, logits)
    m2 = jnp.max(rest, axis=-1, keepdims=True)
    i2 = jnp.min(jnp.where(rest == m2, lane, float(LANES)), axis=-1, keepdims=True)
    e = jnp.exp(m2 - m1)
    g1 = 1.0 / (1.0 + e)
    g2 = e / (1.0 + e)
    oh1 = (lane == i1).astype(F32)
    oh2 = (lane == i2).astype(F32)
    both = oh1 + oh2
    before = carry[0:1, :] + _mm(tri_ref[...], both.astype(BF16), NN, 1, 1)
    rank1 = jnp.sum(before * oh1, axis=-1, keepdims=True)
    rank2 = jnp.sum(before * oh2, axis=-1, keepdims=True)
    tot = carry[0:1, :] + jnp.sum(both, axis=0, keepdims=True)
    carry[...] = jnp.broadcast_to(tot, carry.shape)
    cnt_ref[...] = jnp.broadcast_to(tot, cnt_ref.shape)
    out = jnp.where(lane == 0, g1, 0.0)
    out = jnp.where(lane == 1, g2, out)
    out = jnp.where(lane == 2, i1, out)
    out = jnp.where(lane == 3, i2, out)
    out = jnp.where(lane == 4, rank1, out)
    out = jnp.where(lane == 5, rank2, out)
    route_ref[...] = out


def _outproj(ypc, yr, h, w_bf16, g, router, tm, hn_dtype):
    n, d = h.shape
    n_pc, n_r = ypc.shape[1], yr.shape[1]
    route = router is not None
    rowb = lambda w: pl.BlockSpec((tm, w), lambda i: (i, 0))
    const = lambda i: (0, 0)
    args = [ypc, yr, h, w_bf16, g.reshape(1, d)]
    specs = [rowb(n_pc), rowb(n_r), rowb(d), pl.BlockSpec((d, d), const), pl.BlockSpec((1, d), const)]
    out_specs = [rowb(d), rowb(d)]
    out_shape = [jax.ShapeDtypeStruct((n, d), F32), jax.ShapeDtypeStruct((n, d), hn_dtype)]
    scratch = []
    n_exp = 0
    if route:
        n_exp = router.shape[1]
        rt = jnp.zeros((d, LANES), F32).at[:, :n_exp].set(router)
        ti = jnp.arange(tm)
        tri = (ti[:, None] > ti[None, :]).astype(BF16)
        args += [rt, tri]
        specs += [pl.BlockSpec((d, LANES), const), pl.BlockSpec((tm, tm), const)]
        out_specs += [rowb(LANES), pl.BlockSpec((8, LANES), const)]
        out_shape += [jax.ShapeDtypeStruct((n, LANES), F32), jax.ShapeDtypeStruct((8, LANES), F32)]
        scratch = [pltpu.VMEM((8, LANES), F32)]
    return pl.pallas_call(
        functools.partial(_outproj_kernel, n_pc=n_pc, route=route, n_exp=n_exp),
        grid=(n // tm,),
        in_specs=specs,
        out_specs=out_specs,
        out_shape=out_shape,
        scratch_shapes=scratch,
        compiler_params=_cparams(("arbitrary",)),
        name="outproj_route" if route else "outproj",
    )(*args)


def _ffn_kernel(be_ref, nu_ref, *refs, has_res):
    if has_res:
        x_ref, wg_ref, wu_ref, wd_ref, res_ref, o_ref = refs
    else:
        x_ref, wg_ref, wu_ref, wd_ref, o_ref = refs
    del be_ref
    used = pl.program_id(0) < nu_ref[0]

    @pl.when(used)
    def _():
        x = x_ref[...].astype(BF16)
        gate = jnp.dot(x, wg_ref[0], preferred_element_type=F32)
        up = jnp.dot(x, wu_ref[0], preferred_element_type=F32)
        mid = (_silu(gate) * up).astype(BF16)
        y = jnp.dot(mid, wd_ref[0], preferred_element_type=F32)
        o_ref[...] = y + res_ref[...] if has_res else y

    @pl.when(jnp.logical_not(used))
    def _():
        o_ref[...] = res_ref[...] if has_res else jnp.zeros(o_ref.shape, F32)


def _ffn(x, wg, wu, wd, block_e, n_used, res, tr):
    rows, d = x.shape
    _, _, fdim = wg.shape
    has_res = res is not None
    resident = pl.Buffered(1)
    args = [x, wg, wu, wd]
    specs = [
        pl.BlockSpec((tr, d), lambda i, be, nu: (i, 0)),
        pl.BlockSpec((1, d, fdim), lambda i, be, nu: (be[i], 0, 0), pipeline_mode=resident),
        pl.BlockSpec((1, d, fdim), lambda i, be, nu: (be[i], 0, 0), pipeline_mode=resident),
        pl.BlockSpec((1, fdim, d), lambda i, be, nu: (be[i], 0, 0), pipeline_mode=resident),
    ]
    if has_res:
        args.append(res)
        specs.append(pl.BlockSpec((tr, d), lambda i, be, nu: (i, 0)))
    return pl.pallas_call(
        functools.partial(_ffn_kernel, has_res=has_res),
        grid_spec=pltpu.PrefetchScalarGridSpec(
            num_scalar_prefetch=2,
            grid=(rows // tr,),
            in_specs=specs,
            out_specs=pl.BlockSpec((tr, d), lambda i, be, nu: (i, 0)),
        ),
        out_shape=jax.ShapeDtypeStruct((rows, d), F32),
        compiler_params=_cparams(("arbitrary",)),
        name="ffn_res" if has_res else "ffn_grouped",
    )(block_e, n_used, *args)


def _row_copy(src_ref, s, dst_ref, d, sem):
    return pltpu.make_async_copy(src_ref.at[pl.ds(s, 1), :], dst_ref.at[pl.ds(d, 1), :], sem)


def _dispatch_kernel(dest_ref, x_ref, xd_in_ref, xd_ref, sem, *, tm):
    del xd_in_ref

    for i in range(N_TOP * tm):
        _row_copy(x_ref, i // N_TOP, xd_ref, dest_ref[0, 0, i], sem).start()
    for i in range(N_TOP * tm):
        _row_copy(x_ref, 0, xd_ref, 0, sem).wait()


def _dispatch(x, dest, n_rows, tm):
    n, d = x.shape
    nblk = n // tm
    dest3 = dest.reshape(nblk, 1, N_TOP * tm)
    xd0 = jnp.zeros((n_rows, d), x.dtype)
    return pl.pallas_call(
        functools.partial(_dispatch_kernel, tm=tm),
        grid=(nblk,),
        in_specs=[
            pl.BlockSpec((1, 1, N_TOP * tm), lambda i: (i, 0, 0), memory_space=pltpu.SMEM),
            pl.BlockSpec((tm, d), lambda i: (i, 0)),
            pl.BlockSpec(memory_space=pl.ANY),
        ],
        out_specs=pl.BlockSpec(memory_space=pl.ANY),
        out_shape=jax.ShapeDtypeStruct((n_rows, d), x.dtype),
        scratch_shapes=[pltpu.SemaphoreType.DMA(())],
        input_output_aliases={2: 0},
        compiler_params=_cparams(("arbitrary",)),
        name="moe_dispatch",
    )(dest3, x, xd0)


def _combine_kernel(dest_ref, yd_ref, gate_ref, h_ref, fg_ref, o_ref, buf, sem, *, tm, final_norm):
    for i in range(N_TOP * tm):
        _row_copy(yd_ref, dest_ref[0, 0, i], buf.at[i % N_TOP], i // N_TOP, sem).start()
    for i in range(N_TOP * tm):
        _row_copy(yd_ref, 0, buf.at[0], 0, sem).wait()

    gate = gate_ref[...]
    h = h_ref[...]
    for kk in range(N_TOP):
        h = h + buf[kk] * gate[:, kk:kk + 1]
    if final_norm:
        ms = jnp.mean(h * h, axis=-1, keepdims=True)
        h = h * lax.rsqrt(ms + RMS_EPS) * fg_ref[...]
    o_ref[...] = h


def _combine(yd, dest, gates, h, final_g, tm, final_norm):
    n, d = h.shape
    nblk = n // tm
    dest3 = dest.reshape(nblk, 1, N_TOP * tm)
    return pl.pallas_call(
        functools.partial(_combine_kernel, tm=tm, final_norm=final_norm),
        grid=(nblk,),
        in_specs=[
            pl.BlockSpec((1, 1, N_TOP * tm), lambda i: (i, 0, 0), memory_space=pltpu.SMEM),
            pl.BlockSpec(memory_space=pl.ANY),
            pl.BlockSpec((tm, LANES), lambda i: (i, 0)),
            pl.BlockSpec((tm, d), lambda i: (i, 0)),
            pl.BlockSpec((1, d), lambda i: (0, 0)),
        ],
        out_specs=pl.BlockSpec((tm, d), lambda i: (i, 0)),
        out_shape=jax.ShapeDtypeStruct((n, d), F32),
        scratch_shapes=[pltpu.VMEM((N_TOP, tm, d), F32), pltpu.SemaphoreType.DMA(())],
        compiler_params=_cparams(("arbitrary",)),
        name="moe_combine",
    )(dest3, yd, gates, h, final_g.reshape(1, d))


def _rmsnorm_kernel(x_ref, g_ref, o_ref):
    x = x_ref[...]
    ms = jnp.mean(x * x, axis=-1, keepdims=True)
    o_ref[...] = x * lax.rsqrt(ms + RMS_EPS) * g_ref[...]


def _rmsnorm(x, g, tm):
    n, d = x.shape
    return pl.pallas_call(
        _rmsnorm_kernel,
        grid=(n // tm,),
        in_specs=[pl.BlockSpec((tm, d), lambda i: (i, 0)), pl.BlockSpec((1, d), lambda i: (0, 0))],
        out_specs=pl.BlockSpec((tm, d), lambda i: (i, 0)),
        out_shape=jax.ShapeDtypeStruct((n, d), F32),
        compiler_params=_cparams(("arbitrary",)),
        name="final_rmsnorm",
    )(x, g.reshape(1, d))


def _pick(total, target, mult):
    best = None
    for cand in range(mult, min(total, target) + 1, mult):
        if total % cand == 0:
            best = cand
    assert best is not None, (total, target, mult)
    return best


def kernel(x, norm1_g, w_in, pool_w, pool_scale, conv_w, conv_b, conv_ln_g, conv_ln_b, shift_mu, rwkv_w0, rwkv_w2, rwkv_a0, rwkv_a2, rwkv_g2, rwkv_k_k, rwkv_k_a, rwkv_r_k, rwkv_gn_g, rwkv_gn_b, rwkv_v0, rwkv_v1, rwkv_v2, w_out, norm2_g, ffn_w_gate, ffn_w_up, ffn_w_down, moe_router, moe_w_gate, moe_w_up, moe_w_down, final_g):
    bsz, t, d = x.shape
    depth = w_in.shape[0]
    n = bsz * t
    n_pool = pool_scale.shape[1]
    n_conv = 2 * conv_b.shape[1]
    assert t % CHUNK == 0

    tm = _pick(n, 512, 8)
    tt_pc = _pick(t, 512, HALO)
    tt_prep = _pick(t, 256, CHUNK)
    tt_scan = _pick(t, 128, CHUNK)
    tm_moe = _pick(n, 256, 8)

    h = x.reshape(n, d)
    v_first = None
    for l in range(depth):
        pp, pc, prw = _inproj(h, norm1_g[l], w_in[l].astype(BF16), n_pool, n_conv, tm)
        ypc = _poolconv(pp, pc, pool_w[l], pool_scale[l], conv_w[l], conv_b[l],
                        conv_ln_g[l], conv_ln_b[l], bsz, t, tt_pc)
        rp = dict(mu=shift_mu[l], w0=rwkv_w0[l], w2=rwkv_w2[l], a0=rwkv_a0[l], a2=rwkv_a2[l],
                  g2=rwkv_g2[l], k_k=rwkv_k_k[l], k_a=rwkv_k_a[l], r_k=rwkv_r_k[l])
        vres = None if l == 0 else (rwkv_v0[l - 1], rwkv_v1[l - 1], rwkv_v2[l - 1])
        m, nn, q, y0, bonus, g, v_l = _rwkv_prep(prw, v_first, rp, vres, bsz, t, tt_prep)
        if l == 0:
            v_first = v_l
        yr = _rwkv_scan(m, nn, q, y0, bonus, g, rwkv_gn_g[l], rwkv_gn_b[l], bsz, t, tt_scan)

        i = l // 2
        last = l == depth - 1
        if l % 2 == 0:
            h_mid, hn = _outproj(ypc, yr, h, w_out[l].astype(BF16), norm2_g[l], None, tm, BF16)
            tr = _pick(n, 512, 8)
            nblk = n // tr
            h = _ffn(hn, ffn_w_gate[i][None].astype(BF16), ffn_w_up[i][None].astype(BF16),
                     ffn_w_down[i][None].astype(BF16), jnp.zeros((nblk,), jnp.int32),
                     jnp.full((1,), nblk, jnp.int32), h_mid, tr)
            if last:
                h = _rmsnorm(h, final_g, tm)
        else:
            n_exp = moe_router.shape[2]
            h_mid, hn, route, cnt = _outproj(ypc, yr, h, w_out[l].astype(BF16), norm2_g[l],
                                             moe_router[i], tm, F32)
            gates = route
            top_e = route[:, 2:2 + N_TOP].astype(jnp.int32)
            rank = route[:, 4:4 + N_TOP].astype(jnp.int32)
            counts = cnt[0, :n_exp].astype(jnp.int32)
            tr = _pick(n, 512, 8)
            padded = (counts + tr - 1) // tr * tr
            pad_ends = jnp.cumsum(padded)
            pad_starts = pad_ends - padded
            dest = pad_starts[top_e] + rank
            nblk = (n * N_TOP) // tr + n_exp
            blk_start = jnp.arange(nblk, dtype=jnp.int32) * tr
            block_e = jnp.minimum(jnp.sum((pad_ends[None, :] <= blk_start[:, None]).astype(jnp.int32), axis=1),
                                  n_exp - 1).astype(jnp.int32)
            n_used = (pad_ends[-1] // tr).astype(jnp.int32).reshape(1)
            xd = _dispatch(hn, dest, nblk * tr, tm_moe)
            yd = _ffn(xd, moe_w_gate[i].astype(BF16), moe_w_up[i].astype(BF16),
                      moe_w_down[i].astype(BF16), block_e, n_used, None, tr)
            h = _combine(yd, dest, gates, h_mid, final_g, tm_moe, last)
    return h.reshape(bsz, t, d)
```

```python
import functools

import jax
import jax.numpy as jnp
from jax import lax
from jax.experimental import pallas as pl
from jax.experimental.pallas import tpu as pltpu

F32 = jnp.float32
BF16 = jnp.bfloat16

HEAD = 64
CHUNK = 64
POOL_WINDOWS = (2, 4, 8, 16)
CONV_KERNEL = 31
HALO = 32
RMS_EPS = 1e-6
LN_EPS = 1e-5
GN_EPS = 64e-5
DECAY_SCALE = 0.6065306597126334
N_TOP = 2
LANES = 128
SUBLANES = 8
VMEM_LIMIT = 56 * 1024 * 1024
SCAN_TERMS = 1
PREP_CHUNKS_PER_STEP = 4
DMA_PRIORITIES = 2

NN = ((1,), (0,))
NT = ((1,), (1,))
TN = ((0,), (0,))


def _split(x, n):
    if x.dtype == BF16:
        return [x]
    parts = []
    rem = x
    for i in range(n):
        p = rem.astype(BF16)
        parts.append(p)
        if i + 1 < n:
            rem = rem - p.astype(F32)
    return parts


def _mm(a, b, dims=NN, na=1, nb=1):
    pa = _split(a, na)
    pb = _split(b, nb)
    nmax = max(len(pa), len(pb))
    out = None
    for i, x in enumerate(pa):
        for j, y in enumerate(pb):
            if i + j < nmax:
                t = lax.dot_general(x, y, (dims, ((), ())), preferred_element_type=F32)
                out = t if out is None else out + t
    return out


def _sigmoid(x):
    return 0.5 * jnp.tanh(0.5 * x) + 0.5


def _silu(x):
    return x * _sigmoid(x)


def _cparams(sem):
    return pltpu.CompilerParams(dimension_semantics=sem, vmem_limit_bytes=VMEM_LIMIT)


def _inproj_kernel(x_ref, g_ref, w_ref, op_ref, oc_ref, or_ref, *, n_pool, n_conv):
    x = x_ref[...]
    ms = jnp.mean(x * x, axis=-1, keepdims=True)
    y = x * lax.rsqrt(ms + RMS_EPS) * g_ref[...]
    p = jnp.dot(y.astype(BF16), w_ref[...], preferred_element_type=F32)
    op_ref[...] = p[:, :n_pool]
    oc_ref[...] = p[:, n_pool:n_pool + n_conv]
    or_ref[...] = p[:, n_pool + n_conv:]


def _inproj(h, g, w_bf16, n_pool, n_conv, tm):
    n, d = h.shape
    n_in = w_bf16.shape[1]
    n_rw = n_in - n_pool - n_conv
    return pl.pallas_call(
        functools.partial(_inproj_kernel, n_pool=n_pool, n_conv=n_conv),
        grid=(n // tm,),
        in_specs=[
            pl.BlockSpec((tm, d), lambda i: (i, 0)),
            pl.BlockSpec((1, d), lambda i: (0, 0)),
            pl.BlockSpec((d, n_in), lambda i: (0, 0)),
        ],
        out_specs=[
            pl.BlockSpec((tm, n_pool), lambda i: (i, 0)),
            pl.BlockSpec((tm, n_conv), lambda i: (i, 0)),
            pl.BlockSpec((tm, n_rw), lambda i: (i, 0)),
        ],
        out_shape=[
            jax.ShapeDtypeStruct((n, n_pool), F32),
            jax.ShapeDtypeStruct((n, n_conv), F32),
            jax.ShapeDtypeStruct((n, n_rw), F32),
        ],
        compiler_params=_cparams(("arbitrary",)),
        name="inproj",
    )(h, g.reshape(1, d), w_bf16)


def _poolconv_kernel(pp_ref, pc_ref, pw_ref, ps_ref, wl_ref, cw_ref, cb_ref, lg_ref, lb_ref,
                     o_ref, extp, extc, psum, rot, *, tt, cp, cc):
    j = pl.program_id(1)
    lext = HALO + tt

    @pl.when(j == 0)
    def _():
        extp[0:HALO, :] = jnp.zeros((HALO, cp), F32)
        extc[0:HALO, :] = jnp.zeros((HALO, cc), F32)

    u = pp_ref[...]
    extp[HALO:HALO + tt, :] = u
    wl = wl_ref[...]
    sel = jnp.zeros_like(u)
    cur, w, lo = extp, 1, 0
    for idx, win in enumerate(POOL_WINDOWS):
        assert win == 2 * w
        lo = -(-(lo + w) // SUBLANES) * SUBLANES
        assert lo <= HALO
        val = cur[lo:lext, :] + cur[pl.ds(lo - w, lext - lo), :]
        if idx + 1 < len(POOL_WINDOWS):
            psum[idx, lo:lext, :] = val
            cur = psum.at[idx]
        sel = jnp.where(wl == float(win), val[HALO - lo:], sel)
        w = win
    pos = (j * tt + lax.broadcasted_iota(jnp.int32, (tt, 1), 0) + 1).astype(F32)
    inv = jnp.where(pos >= wl, 1.0 / wl, 1.0 / pos)
    d = sel * inv - u
    yp = _mm(d, pw_ref[...], NN, 2, 2) * ps_ref[...]
    extp[0:HALO, :] = extp[tt:tt + HALO, :]

    pc = pc_ref[...]
    uc = pc[:, :cc] * _sigmoid(pc[:, cc:])
    extc[HALO:HALO + tt, :] = uc
    nrot = lext - SUBLANES
    for r in range(1, SUBLANES):
        rot[r - 1] = extc[pl.ds(r, nrot), :]
    acc = jnp.zeros((tt, cc), F32)
    for t in range(CONV_KERNEL):
        q, r = divmod(HALO - (CONV_KERNEL - 1) + t, SUBLANES)
        if r == 0:
            tap = extc[pl.ds(SUBLANES * q, tt), :]
        else:
            tap = rot[r - 1, pl.ds(SUBLANES * q, tt), :]
        acc = acc + cw_ref[t:t + 1, :] * tap
    extc[0:HALO, :] = extc[tt:tt + HALO, :]
    yf = acc + cb_ref[...]
    mu = jnp.mean(yf, axis=-1, keepdims=True)
    yc = yf - mu
    var = jnp.mean(yc * yc, axis=-1, keepdims=True)
    yn = yc * lax.rsqrt(var + LN_EPS) * lg_ref[...] + lb_ref[...]
    o_ref[:, :cp] = yp
    o_ref[:, cp:] = _silu(yn)


def _poolconv(pp, pc, pool_w, pool_scale, conv_w, conv_b, ln_g, ln_b, bsz, t, tt):
    n, cp = pp.shape
    cc = pc.shape[1] // 2
    ng, cg, _ = pool_w.shape
    pw = jnp.zeros((cp, cp), F32)
    for gi in range(ng):
        pw = pw.at[gi * cg:(gi + 1) * cg, gi * cg:(gi + 1) * cg].set(pool_w[gi])
    wl = jnp.repeat(jnp.asarray(POOL_WINDOWS, F32), cg).reshape(1, cp)
    nt = t // tt
    row = lambda b, j: (b * nt + j, 0)
    const = lambda b, j: (0, 0)
    return pl.pallas_call(
        functools.partial(_poolconv_kernel, tt=tt, cp=cp, cc=cc),
        grid=(bsz, nt),
        in_specs=[
            pl.BlockSpec((tt, cp), row),
            pl.BlockSpec((tt, 2 * cc), row),
            pl.BlockSpec((cp, cp), const),
            pl.BlockSpec((1, cp), const),
            pl.BlockSpec((1, cp), const),
            pl.BlockSpec((CONV_KERNEL, cc), const),
            pl.BlockSpec((1, cc), const),
            pl.BlockSpec((1, cc), const),
            pl.BlockSpec((1, cc), const),
        ],
        out_specs=pl.BlockSpec((tt, cp + cc), row),
        out_shape=jax.ShapeDtypeStruct((n, cp + cc), F32),
        scratch_shapes=[pltpu.VMEM((tt + HALO, cp), F32), pltpu.VMEM((tt + HALO, cc), F32),
                        pltpu.VMEM((len(POOL_WINDOWS) - 1, tt + HALO, cp), F32),
                        pltpu.VMEM((SUBLANES - 1, tt + HALO - SUBLANES, cc), F32)],
        compiler_params=_cparams(("arbitrary", "arbitrary")),
        name="poolconv",
    )(pp, pc, pw, pool_scale.reshape(1, cp), wl, conv_w, conv_b.reshape(1, cc),
      ln_g.reshape(1, cc), ln_b.reshape(1, cc))


def _rwkv_prep_kernel(*refs, tt, wd, r_dec, r_icl, r_gate, has_vres):
    it = iter(refs)
    c_ref = next(it)
    vf_ref = next(it) if has_vres else None
    mu_ref, w0_ref, w2_ref, a0_ref, a2_ref, g2_ref, kk_ref, ka_ref, rk_ref = (next(it) for _ in range(9))
    if has_vres:
        v0_ref, v1_ref, v2_ref = (next(it) for _ in range(3))
    ones_ref, tril_ref, blk_ref = next(it), next(it), next(it)
    m_ref, n_ref, q_ref, y0_ref, bonus_ref, g_ref, v_ref = (next(it) for _ in range(7))
    ext = next(it)
    s_at, s_rt, s_bt, s_kt, s_btp, s_ktp, s_pc, s_v = (next(it) for _ in range(8))

    j = pl.program_id(1)
    ncols = c_ref.shape[1]

    @pl.when(j == 0)
    def _():
        ext[0:8, :] = jnp.zeros((8, ncols), F32)

    c = c_ref[...]
    ext[8:8 + tt, :] = c
    prev = ext[pl.ds(7, tt), :]
    ext[0:8, :] = ext[tt:tt + 8, :]
    cf = c + mu_ref[...] * (prev - c)

    r = cf[:, :wd]
    k = cf[:, wd:2 * wd]
    v = cf[:, 2 * wd:3 * wd]
    o = 3 * wd
    xw = cf[:, o:o + r_dec]
    o += r_dec
    xa = cf[:, o:o + r_icl]
    o += r_icl
    xg = cf[:, o:o + r_gate]

    z = w0_ref[...] + _mm(jnp.tanh(xw), w2_ref[...], NN, 2, 2)
    lw = -DECAY_SCALE * _sigmoid(z)
    a = _sigmoid(a0_ref[...] + _mm(xa, a2_ref[...], NN, 2, 2))
    g = _mm(_sigmoid(xg), g2_ref[...], NN, 2, 2)
    if has_vres:
        lo = _mm(_mm(v, v1_ref[...], NN, 2, 2), v2_ref[...], NN, 2, 2)
        v = v + (vf_ref[...] - v) * _sigmoid(v0_ref[...] + lo)

    ones = ones_ref[...]
    kkp = k * kk_ref[...]
    kk = kkp * lax.rsqrt(jnp.maximum(_mm(kkp * kkp, ones, NN, 2, 1), 1e-24))
    k2 = k * (1.0 + (a - 1.0) * ka_ref[...])
    bonus_ref[...] = _mm(r * k2 * rk_ref[...], ones, NN, 2, 1) * v
    g_ref[...] = g
    v_ref[...] = v

    cl = _mm(tril_ref[...], lw, NN, 1, 3)
    lc = _mm(blk_ref[...], lw, NN, 1, 3)
    e_neg = jnp.exp(-cl)
    e_rem = jnp.exp(lc - cl)
    s_at[...] = -kk * jnp.exp(cl - lw)
    s_rt[...] = r * jnp.exp(cl)
    s_bt[...] = kk * a * e_neg
    s_kt[...] = k2 * e_neg
    s_btp[...] = kk * a * e_rem
    s_ktp[...] = k2 * e_rem
    s_pc[...] = jnp.exp(lc)
    s_v[...] = v

    ri = lax.broadcasted_iota(jnp.int32, (CHUNK, CHUNK), 0)
    ci = lax.broadcasted_iota(jnp.int32, (CHUNK, CHUNK), 1)
    strict = ri > ci
    incl = ri >= ci
    eye = (ri == ci).astype(F32)

    def chunk_body(it, carry):
        units = []
        for cc in range(PREP_CHUNKS_PER_STEP):
            rs = pl.ds(pl.multiple_of((it * PREP_CHUNKS_PER_STEP + cc) * CHUNK, CHUNK), CHUNK)
            units += [(rs, slice(h * HEAD, (h + 1) * HEAD)) for h in range(wd // HEAD)]
        nu = len(units)
        sp = SCAN_TERMS
        at = [s_at[u] for u in units]
        rt = [s_rt[u] for u in units]
        vv = [s_v[u] for u in units]
        aa = [_mm(jnp.concatenate([at[i], rt[i]], axis=0),
                  jnp.concatenate([s_bt[u], s_kt[u]], axis=0), NT, sp, sp)
              for i, u in enumerate(units)]
        p = [jnp.where(strict, a[:CHUNK, :CHUNK], 0.0) for a in aa]
        a_ak = [jnp.where(strict, a[:CHUNK, CHUNK:], 0.0) for a in aa]
        a_rb = [jnp.where(incl, a[CHUNK:, :CHUNK], 0.0) for a in aa]
        a_rk = [jnp.where(incl, a[CHUNK:, CHUNK:], 0.0) for a in aa]
        av = [_mm(jnp.concatenate([a_ak[i], a_rk[i]], axis=0), vv[i], NN, sp, sp) for i in range(nu)]
        yv = [a[CHUNK:] for a in av]
        x = [jnp.concatenate([at[i], av[i][:CHUNK]], axis=1) for i in range(nu)]
        nsteps = CHUNK.bit_length() - 1
        for s in range(nsteps):
            if s + 1 < nsteps:
                px = [_mm(pi, jnp.concatenate([xi, pi], axis=1), NN, sp, sp) for pi, xi in zip(p, x)]
                x = [xi + r[:, :2 * HEAD] for xi, r in zip(x, px)]
                p = [r[:, 2 * HEAD:] for r in px]
            else:
                x = [xi + _mm(pi, xi, NN, sp, sp) for pi, xi in zip(p, x)]
        fin = [_mm(jnp.concatenate([jnp.concatenate([s_btp[u], s_ktp[u]], axis=1).T, a_rb[i]], axis=0),
                   jnp.concatenate([x[i], vv[i]], axis=1), NN, sp, sp)
               for i, u in enumerate(units)]
        for i, u in enumerate(units):
            f = fin[i]
            m_ref[u] = eye * s_pc[u] + f[:HEAD, :HEAD]
            n_ref[u] = f[:HEAD, HEAD:2 * HEAD] + f[HEAD:2 * HEAD, 2 * HEAD:]
            q_ref[u] = rt[i] + f[2 * HEAD:, :HEAD]
            y0_ref[u] = f[2 * HEAD:, HEAD:2 * HEAD] + yv[i]
        return carry

    lax.fori_loop(0, tt // (CHUNK * PREP_CHUNKS_PER_STEP), chunk_body, 0)


def _rwkv_prep(c, v_first, p, vres, bsz, t, tt):
    n, ncols = c.shape
    wd = p["w0"].shape[0]
    r_dec, r_icl, r_gate = p["w2"].shape[0], p["a2"].shape[0], p["g2"].shape[0]
    has_vres = vres is not None
    nt = t // tt
    row = lambda b, j: (b * nt + j, 0)
    const = lambda b, j: (0, 0)
    vec = lambda a: a.reshape(1, -1)
    hid = jnp.arange(wd) // HEAD
    ones = (hid[:, None] == hid[None, :]).astype(BF16)
    ti = jnp.arange(tt)
    same = ti[:, None] // CHUNK == ti[None, :] // CHUNK
    tril = (same & (ti[:, None] >= ti[None, :])).astype(BF16)
    blk = same.astype(BF16)

    args = [c]
    specs = [pl.BlockSpec((tt, ncols), row)]
    if has_vres:
        args.append(v_first)
        specs.append(pl.BlockSpec((tt, wd), row))
    small = [vec(p["mu"]), vec(p["w0"]), p["w2"], vec(p["a0"]), p["a2"], p["g2"],
             vec(p["k_k"]), vec(p["k_a"]), vec(p["r_k"])]
    if has_vres:
        small += [vec(vres[0]), vres[1], vres[2]]
    small += [ones, tril, blk]
    for s in small:
        args.append(s)
        specs.append(pl.BlockSpec(s.shape, const))
    out_spec = pl.BlockSpec((tt, wd), row)
    out_sds = jax.ShapeDtypeStruct((n, wd), F32)
    return pl.pallas_call(
        functools.partial(_rwkv_prep_kernel, tt=tt, wd=wd, r_dec=r_dec, r_icl=r_icl,
                          r_gate=r_gate, has_vres=has_vres),
        grid=(bsz, nt),
        in_specs=specs,
        out_specs=[out_spec] * 7,
        out_shape=[out_sds] * 7,
        scratch_shapes=[pltpu.VMEM((tt + 8, ncols), F32)] + [pltpu.VMEM((tt, wd), F32)] * 8,
        compiler_params=_cparams(("arbitrary", "arbitrary")),
        name="rwkv_prep",
    )(*args)


def _rwkv_scan_kernel(m_ref, n_ref, q_ref, y0_ref, bonus_ref, g_ref, gg_ref, gb_ref, ones_ref,
                      o_ref, z_ref, y_ref, *, bsz, tt, wd):
    j = pl.program_id(0)
    nh = wd // HEAD

    @pl.when(j == 0)
    def _():
        z_ref[...] = jnp.zeros(z_ref.shape, F32)

    for ch in range(tt // CHUNK):
        rs = slice(ch * CHUNK, (ch + 1) * CHUNK)
        for b in range(bsz):
            for h in range(nh):
                hs = slice(h * HEAD, (h + 1) * HEAD)
                z = z_ref[b * nh + h]
                qm = jnp.concatenate([q_ref[b, rs, hs], m_ref[b, rs, hs]], axis=0)
                res = _mm(qm, z, NN, 1, 2)
                y_ref[b, rs, hs] = res[:CHUNK] + y0_ref[b, rs, hs]
                z_ref[b * nh + h] = res[CHUNK:] + n_ref[b, rs, hs]

    ones = ones_ref[...]
    for b in range(bsz):
        y = y_ref[b]
        ym = _mm(y, ones, NN, 2, 1) * (1.0 / HEAD)
        yc = y - ym
        yv = _mm(yc * yc, ones, NN, 2, 1) * (1.0 / HEAD)
        yn = yc * lax.rsqrt(yv + GN_EPS) * gg_ref[...] + gb_ref[...]
        o_ref[b] = (yn + bonus_ref[b]) * g_ref[b]


def _rwkv_scan(m, nn, q, y0, bonus, g, gn_g, gn_b, bsz, t, tt):
    n, wd = m.shape
    nh = wd // HEAD
    hid = jnp.arange(wd) // HEAD
    ones = (hid[:, None] == hid[None, :]).astype(BF16)
    r3 = lambda a: a.reshape(bsz, t, wd)
    blk = pl.BlockSpec((bsz, tt, wd), lambda j: (0, j, 0))
    const = lambda j: (0, 0)
    out = pl.pallas_call(
        functools.partial(_rwkv_scan_kernel, bsz=bsz, tt=tt, wd=wd),
        grid=(t // tt,),
        in_specs=[blk] * 6 + [pl.BlockSpec((1, wd), const), pl.BlockSpec((1, wd), const),
                              pl.BlockSpec((wd, wd), const)],
        out_specs=blk,
        out_shape=jax.ShapeDtypeStruct((bsz, t, wd), F32),
        scratch_shapes=[pltpu.VMEM((bsz * nh, HEAD, HEAD), F32), pltpu.VMEM((bsz, tt, wd), F32)],
        compiler_params=_cparams(("arbitrary",)),
        name="rwkv_scan",
    )(r3(m), r3(nn), r3(q), r3(y0), r3(bonus), r3(g), gn_g.reshape(1, wd), gn_b.reshape(1, wd), ones)
    return out.reshape(n, wd)


def _outproj_kernel(*refs, n_pc, route, n_exp):
    if route:
        (ypc_ref, yr_ref, h_ref, w_ref, g_ref, rt_ref, tri_ref,
         ho_ref, hn_ref, route_ref, cnt_ref, carry) = refs
    else:
        ypc_ref, yr_ref, h_ref, w_ref, g_ref, ho_ref, hn_ref = refs
    mix = (jnp.dot(ypc_ref[...].astype(BF16), w_ref[:n_pc, :], preferred_element_type=F32)
           + jnp.dot(yr_ref[...].astype(BF16), w_ref[n_pc:, :], preferred_element_type=F32))
    h = h_ref[...] + mix
    ho_ref[...] = h
    ms = jnp.mean(h * h, axis=-1, keepdims=True)
    hn = h * lax.rsqrt(ms + RMS_EPS) * g_ref[...]
    hn_ref[...] = hn.astype(hn_ref.dtype)
    if not route:
        return

    i = pl.program_id(0)

    @pl.when(i == 0)
    def _():
        carry[...] = jnp.zeros(carry.shape, F32)

    tm = hn.shape[0]
    logits = _mm(hn, rt_ref[...], NN, 2, 2)
    lane = lax.broadcasted_iota(jnp.int32, (tm, LANES), 1).astype(F32)
    neg = jnp.float32(-jnp.inf)
    logits = jnp.where(lane < n_exp, logits, neg)
    m1 = jnp.max(logits, axis=-1, keepdims=True)
    i1 = jnp.min(jnp.where(logits == m1, lane, float(LANES)), axis=-1, keepdims=True)
    rest = jnp.where(lane == i1, neg, logits)
    m2 = jnp.max(rest, axis=-1, keepdims=True)
    i2 = jnp.min(jnp.where(rest == m2, lane, float(LANES)), axis=-1, keepdims=True)
    e = jnp.exp(m2 - m1)
    g1 = 1.0 / (1.0 + e)
    g2 = e / (1.0 + e)
    oh1 = (lane == i1).astype(F32)
    oh2 = (lane == i2).astype(F32)
    both = oh1 + oh2
    before = carry[0:1, :] + _mm(tri_ref[...], both.astype(BF16), NN, 1, 1)
    rank1 = jnp.sum(before * oh1, axis=-1, keepdims=True)
    rank2 = jnp.sum(before * oh2, axis=-1, keepdims=True)
    tot = carry[0:1, :] + jnp.sum(both, axis=0, keepdims=True)
    carry[...] = jnp.broadcast_to(tot, carry.shape)
    cnt_ref[...] = jnp.broadcast_to(tot, cnt_ref.shape)
    out = jnp.where(lane == 0, g1, 0.0)
    out = jnp.where(lane == 1, g2, out)
    out = jnp.where(lane == 2, i1, out)
    out = jnp.where(lane == 3, i2, out)
    out = jnp.where(lane == 4, rank1, out)
    out = jnp.where(lane == 5, rank2, out)
    route_ref[...] = out


def _outproj(ypc, yr, h, w_bf16, g, router, tm, hn_dtype):
    n, d = h.shape
    n_pc, n_r = ypc.shape[1], yr.shape[1]
    route = router is not None
    rowb = lambda w: pl.BlockSpec((tm, w), lambda i: (i, 0))
    const = lambda i: (0, 0)
    args = [ypc, yr, h, w_bf16, g.reshape(1, d)]
    specs = [rowb(n_pc), rowb(n_r), rowb(d), pl.BlockSpec((d, d), const), pl.BlockSpec((1, d), const)]
    out_specs = [rowb(d), rowb(d)]
    out_shape = [jax.ShapeDtypeStruct((n, d), F32), jax.ShapeDtypeStruct((n, d), hn_dtype)]
    scratch = []
    n_exp = 0
    if route:
        n_exp = router.shape[1]
        rt = jnp.zeros((d, LANES), F32).at[:, :n_exp].set(router)
        ti = jnp.arange(tm)
        tri = (ti[:, None] > ti[None, :]).astype(BF16)
        args += [rt, tri]
        specs += [pl.BlockSpec((d, LANES), const), pl.BlockSpec((tm, tm), const)]
        out_specs += [rowb(LANES), pl.BlockSpec((8, LANES), const)]
        out_shape += [jax.ShapeDtypeStruct((n, LANES), F32), jax.ShapeDtypeStruct((8, LANES), F32)]
        scratch = [pltpu.VMEM((8, LANES), F32)]
    return pl.pallas_call(
        functools.partial(_outproj_kernel, n_pc=n_pc, route=route, n_exp=n_exp),
        grid=(n // tm,),
        in_specs=specs,
        out_specs=out_specs,
        out_shape=out_shape,
        scratch_shapes=scratch,
        compiler_params=_cparams(("arbitrary",)),
        name="outproj_route" if route else "outproj",
    )(*args)


def _ffn_kernel(be_ref, nu_ref, *refs, has_res):
    if has_res:
        x_ref, wg_ref, wu_ref, wd_ref, res_ref, o_ref = refs
    else:
        x_ref, wg_ref, wu_ref, wd_ref, o_ref = refs
    del be_ref
    used = pl.program_id(0) < nu_ref[0]

    @pl.when(used)
    def _():
        x = x_ref[...].astype(BF16)
        gate = jnp.dot(x, wg_ref[0], preferred_element_type=F32)
        up = jnp.dot(x, wu_ref[0], preferred_element_type=F32)
        mid = (_silu(gate) * up).astype(BF16)
        y = jnp.dot(mid, wd_ref[0], preferred_element_type=F32)
        o_ref[...] = y + res_ref[...] if has_res else y

    @pl.when(jnp.logical_not(used))
    def _():
        o_ref[...] = res_ref[...] if has_res else jnp.zeros(o_ref.shape, F32)


def _ffn(x, wg, wu, wd, block_e, n_used, res, tr):
    rows, d = x.shape
    _, _, fdim = wg.shape
    has_res = res is not None
    resident = pl.Buffered(1)
    args = [x, wg, wu, wd]
    specs = [
        pl.BlockSpec((tr, d), lambda i, be, nu: (i, 0)),
        pl.BlockSpec((1, d, fdim), lambda i, be, nu: (be[i], 0, 0), pipeline_mode=resident),
        pl.BlockSpec((1, d, fdim), lambda i, be, nu: (be[i], 0, 0), pipeline_mode=resident),
        pl.BlockSpec((1, fdim, d), lambda i, be, nu: (be[i], 0, 0), pipeline_mode=resident),
    ]
    if has_res:
        args.append(res)
        specs.append(pl.BlockSpec((tr, d), lambda i, be, nu: (i, 0)))
    return pl.pallas_call(
        functools.partial(_ffn_kernel, has_res=has_res),
        grid_spec=pltpu.PrefetchScalarGridSpec(
            num_scalar_prefetch=2,
            grid=(rows // tr,),
            in_specs=specs,
            out_specs=pl.BlockSpec((tr, d), lambda i, be, nu: (i, 0)),
        ),
        out_shape=jax.ShapeDtypeStruct((rows, d), F32),
        compiler_params=_cparams(("arbitrary",)),
        name="ffn_res" if has_res else "ffn_grouped",
    )(block_e, n_used, *args)


def _row_copy(src_ref, s, dst_ref, d, sem):
    return pltpu.make_async_copy(src_ref.at[pl.ds(s, 1), :], dst_ref.at[pl.ds(d, 1), :], sem)


def _dispatch_kernel(dest_ref, x_ref, xd_in_ref, xd_ref, sem, *, tm):
    del xd_in_ref

    for i in range(N_TOP * tm):
        _row_copy(x_ref, i // N_TOP, xd_ref, dest_ref[0, 0, i], sem).start(priority=i % DMA_PRIORITIES)
    for i in range(N_TOP * tm):
        _row_copy(x_ref, 0, xd_ref, 0, sem).wait()


def _dispatch(x, dest, n_rows, tm):
    n, d = x.shape
    nblk = n // tm
    dest3 = dest.reshape(nblk, 1, N_TOP * tm)
    xd0 = jnp.zeros((n_rows, d), x.dtype)
    return pl.pallas_call(
        functools.partial(_dispatch_kernel, tm=tm),
        grid=(nblk,),
        in_specs=[
            pl.BlockSpec((1, 1, N_TOP * tm), lambda i: (i, 0, 0), memory_space=pltpu.SMEM),
            pl.BlockSpec((tm, d), lambda i: (i, 0)),
            pl.BlockSpec(memory_space=pl.ANY),
        ],
        out_specs=pl.BlockSpec(memory_space=pl.ANY),
        out_shape=jax.ShapeDtypeStruct((n_rows, d), x.dtype),
        scratch_shapes=[pltpu.SemaphoreType.DMA(())],
        input_output_aliases={2: 0},
        compiler_params=_cparams(("arbitrary",)),
        name="moe_dispatch",
    )(dest3, x, xd0)


def _combine_kernel(dest_ref, yd_ref, gate_ref, h_ref, fg_ref, o_ref, buf, sem, *, tm, final_norm):
    for i in range(N_TOP * tm):
        _row_copy(yd_ref, dest_ref[0, 0, i], buf.at[i % N_TOP], i // N_TOP, sem).start(
            priority=i % DMA_PRIORITIES)
    for i in range(N_TOP * tm):
        _row_copy(yd_ref, 0, buf.at[0], 0, sem).wait()

    gate = gate_ref[...]
    h = h_ref[...]
    for kk in range(N_TOP):
        h = h + buf[kk] * gate[:, kk:kk + 1]
    if final_norm:
        ms = jnp.mean(h * h, axis=-1, keepdims=True)
        h = h * lax.rsqrt(ms + RMS_EPS) * fg_ref[...]
    o_ref[...] = h


def _combine(yd, dest, gates, h, final_g, tm, final_norm):
    n, d = h.shape
    nblk = n // tm
    dest3 = dest.reshape(nblk, 1, N_TOP * tm)
    return pl.pallas_call(
        functools.partial(_combine_kernel, tm=tm, final_norm=final_norm),
        grid=(nblk,),
        in_specs=[
            pl.BlockSpec((1, 1, N_TOP * tm), lambda i: (i, 0, 0), memory_space=pltpu.SMEM),
            pl.BlockSpec(memory_space=pl.ANY),
            pl.BlockSpec((tm, LANES), lambda i: (i, 0)),
            pl.BlockSpec((tm, d), lambda i: (i, 0)),
            pl.BlockSpec((1, d), lambda i: (0, 0)),
        ],
        out_specs=pl.BlockSpec((tm, d), lambda i: (i, 0)),
        out_shape=jax.ShapeDtypeStruct((n, d), F32),
        scratch_shapes=[pltpu.VMEM((N_TOP, tm, d), F32), pltpu.SemaphoreType.DMA(())],
        compiler_params=_cparams(("arbitrary",)),
        name="moe_combine",
    )(dest3, yd, gates, h, final_g.reshape(1, d))


def _rmsnorm_kernel(x_ref, g_ref, o_ref):
    x = x_ref[...]
    ms = jnp.mean(x * x, axis=-1, keepdims=True)
    o_ref[...] = x * lax.rsqrt(ms + RMS_EPS) * g_ref[...]


def _rmsnorm(x, g, tm):
    n, d = x.shape
    return pl.pallas_call(
        _rmsnorm_kernel,
        grid=(n // tm,),
        in_specs=[pl.BlockSpec((tm, d), lambda i: (i, 0)), pl.BlockSpec((1, d), lambda i: (0, 0))],
        out_specs=pl.BlockSpec((tm, d), lambda i: (i, 0)),
        out_shape=jax.ShapeDtypeStruct((n, d), F32),
        compiler_params=_cparams(("arbitrary",)),
        name="final_rmsnorm",
    )(x, g.reshape(1, d))


def _pick(total, target, mult):
    best = None
    for cand in range(mult, min(total, target) + 1, mult):
        if total % cand == 0:
            best = cand
    assert best is not None, (total, target, mult)
    return best


def kernel(x, norm1_g, w_in, pool_w, pool_scale, conv_w, conv_b, conv_ln_g, conv_ln_b, shift_mu, rwkv_w0, rwkv_w2, rwkv_a0, rwkv_a2, rwkv_g2, rwkv_k_k, rwkv_k_a, rwkv_r_k, rwkv_gn_g, rwkv_gn_b, rwkv_v0, rwkv_v1, rwkv_v2, w_out, norm2_g, ffn_w_gate, ffn_w_up, ffn_w_down, moe_router, moe_w_gate, moe_w_up, moe_w_down, final_g):
    bsz, t, d = x.shape
    depth = w_in.shape[0]
    n = bsz * t
    n_pool = pool_scale.shape[1]
    n_conv = 2 * conv_b.shape[1]
    assert t % CHUNK == 0

    tm = _pick(n, 512, 8)
    tt_pc = _pick(t, 512, HALO)
    tt_prep = _pick(t, 256, CHUNK)
    tt_scan = _pick(t, 128, CHUNK)
    tm_moe = _pick(n, 256, 8)

    h = x.reshape(n, d)
    v_first = None
    for l in range(depth):
        pp, pc, prw = _inproj(h, norm1_g[l], w_in[l].astype(BF16), n_pool, n_conv, tm)
        ypc = _poolconv(pp, pc, pool_w[l], pool_scale[l], conv_w[l], conv_b[l],
                        conv_ln_g[l], conv_ln_b[l], bsz, t, tt_pc)
        rp = dict(mu=shift_mu[l], w0=rwkv_w0[l], w2=rwkv_w2[l], a0=rwkv_a0[l], a2=rwkv_a2[l],
                  g2=rwkv_g2[l], k_k=rwkv_k_k[l], k_a=rwkv_k_a[l], r_k=rwkv_r_k[l])
        vres = None if l == 0 else (rwkv_v0[l - 1], rwkv_v1[l - 1], rwkv_v2[l - 1])
        m, nn, q, y0, bonus, g, v_l = _rwkv_prep(prw, v_first, rp, vres, bsz, t, tt_prep)
        if l == 0:
            v_first = v_l
        yr = _rwkv_scan(m, nn, q, y0, bonus, g, rwkv_gn_g[l], rwkv_gn_b[l], bsz, t, tt_scan)

        i = l // 2
        last = l == depth - 1
        if l % 2 == 0:
            h_mid, hn = _outproj(ypc, yr, h, w_out[l].astype(BF16), norm2_g[l], None, tm, BF16)
            tr = _pick(n, 512, 8)
            nblk = n // tr
            h = _ffn(hn, ffn_w_gate[i][None].astype(BF16), ffn_w_up[i][None].astype(BF16),
                     ffn_w_down[i][None].astype(BF16), jnp.zeros((nblk,), jnp.int32),
                     jnp.full((1,), nblk, jnp.int32), h_mid, tr)
            if last:
                h = _rmsnorm(h, final_g, tm)
        else:
            n_exp = moe_router.shape[2]
            h_mid, hn, route, cnt = _outproj(ypc, yr, h, w_out[l].astype(BF16), norm2_g[l],
                                             moe_router[i], tm, F32)
            gates = route
            top_e = route[:, 2:2 + N_TOP].astype(jnp.int32)
            rank = route[:, 4:4 + N_TOP].astype(jnp.int32)
            counts = cnt[0, :n_exp].astype(jnp.int32)
            tr = _pick(n, 512, 8)
            padded = (counts + tr - 1) // tr * tr
            pad_ends = jnp.cumsum(padded)
            pad_starts = pad_ends - padded
            dest = pad_starts[top_e] + rank
            nblk = (n * N_TOP) // tr + n_exp
            blk_start = jnp.arange(nblk, dtype=jnp.int32) * tr
            block_e = jnp.minimum(jnp.sum((pad_ends[None, :] <= blk_start[:, None]).astype(jnp.int32), axis=1),
                                  n_exp - 1).astype(jnp.int32)
            n_used = (pad_ends[-1] // tr).astype(jnp.int32).reshape(1)
            xd = _dispatch(hn, dest, nblk * tr, tm_moe)
            yd = _ffn(xd, moe_w_gate[i].astype(BF16), moe_w_up[i].astype(BF16),
                      moe_w_down[i].astype(BF16), block_e, n_used, None, tr)
            h = _combine(yd, dest, gates, h_mid, final_g, tm_moe, last)
    return h.reshape(bsz, t, d)
```
